```python
import math
import jax, jax.numpy as jnp
from jax import lax
import numpy as np

D_MODEL = 2048
BATCH = 2
SEQ = 4096
DEPTH = 4
DEC_BATCH = 8
DEC_SEQ = 4
PAST_LEN = 16384
PAGE_SIZE = 128

N_AB_LAYERS = DEPTH // 2
N_C_LAYERS = DEPTH - N_AB_LAYERS
EPS = 1e-6
A_WIDTH = D_MODEL // 2
A_GROUPS = 8
A_GW = A_WIDTH // A_GROUPS
A_CHUNK = 128
HEAD_DIM = 128
B_HEADS = 4
B_CONFIGS = ((128, 1), (512, 4), (2048, 16))
B_N_GROUPS = len(B_CONFIGS)
B_QKV = 3 * B_N_GROUPS * B_HEADS * HEAD_DIM
B_OUT = B_HEADS * HEAD_DIM
AB_IN = 2 * A_WIDTH + B_QKV
AB_OUT = A_WIDTH + B_OUT
ROPE_THETA = 10000.0
C_D_INNER = 2 * D_MODEL
C_HEAD_DIM = 64
C_HEADS = C_D_INNER // C_HEAD_DIM
C_GROUPS = 8
C_D_STATE = 128
C_CONV = 4
C_CONV_DIM = C_D_INNER + 2 * C_GROUPS * C_D_STATE
C_IN = C_D_INNER + C_CONV_DIM + C_HEADS
C_CHUNK = 128
E_GROUPS = 8
E_PER_GROUP = 8
N_EXPERTS = E_GROUPS * E_PER_GROUP
E_TOPK = 2
E_FF = D_MODEL // 4
MOE_BLOCK = 128

kernel_name = 'hybrid_gmlp_dilated_ssd_hmoe_step'


def rmsnorm(x, g):
    xf = x.astype(jnp.float32)
    y = xf * lax.rsqrt(jnp.mean(xf * xf, -1, keepdims=True) + EPS)
    return (y * g.astype(jnp.float32)).astype(x.dtype)


def layernorm(x, g, b):
    xf = x.astype(jnp.float32)
    xc = xf - jnp.mean(xf, -1, keepdims=True)
    y = xc * lax.rsqrt(jnp.mean(xc * xc, -1, keepdims=True) + EPS)
    return (y * g.astype(jnp.float32) + b.astype(jnp.float32)).astype(x.dtype)


def rope(x, pos):
    half = HEAD_DIM // 2
    inv = 1.0 / (ROPE_THETA ** (jnp.arange(half, dtype=jnp.float32) / half))
    ang = pos.astype(jnp.float32)[:, None] * inv[None, :]
    cos = jnp.cos(ang)[None, :, None, :]
    sin = jnp.sin(ang)[None, :, None, :]
    xf = x.astype(jnp.float32)
    x1, x2 = xf[..., :half], xf[..., half:]
    return jnp.concatenate([x1 * cos - x2 * sin, x2 * cos + x1 * sin], -1).astype(x.dtype)


def chunk_spatial_gate(v, w_s, b_s):
    b, T, _ = v.shape
    n_c = -(-T // A_CHUNK)
    vp = jnp.pad(v, ((0, 0), (0, n_c * A_CHUNK - T), (0, 0)))
    vp = vp.reshape(b, n_c, A_CHUNK, A_GROUPS, A_GW)
    causal = jnp.tril(jnp.ones((A_CHUNK, A_CHUNK), dtype=bool))
    w = jnp.where(causal[None], w_s, 0).astype(v.dtype)
    mixed = jnp.einsum('gts,bcsgd->bctgd', w, vp) + jnp.transpose(b_s).astype(v.dtype)[None, None, :, :, None]
    return mixed.reshape(b, n_c * A_CHUNK, A_WIDTH)[:, :T]


def dilated_attn_prompt(q, k, v, window, dil):
    b, T, H, Dh = q.shape
    band = window // dil
    L = -(-T // dil)
    n_blk = -(-L // band)
    Tp = n_blk * band * dil

    def to_blocks(t):
        t = jnp.pad(t, ((0, 0), (0, Tp - T), (0, 0), (0, 0)))
        t = t.reshape(b, n_blk * band, dil, H, Dh).transpose(0, 2, 1, 3, 4)
        return t.reshape(b, dil, n_blk, band, H, Dh)

    def with_prev(t):
        prev = jnp.pad(t, ((0, 0), (0, 0), (1, 0), (0, 0), (0, 0), (0, 0)))[:, :, :-1]
        return jnp.concatenate([prev, t], axis=3)

    qb = to_blocks(q)
    kk = with_prev(to_blocks(k))
    vv = with_prev(to_blocks(v))
    s = jnp.einsum('brnqhd,brnkhd->brnhqk', qb, kk, preferred_element_type=jnp.float32) * (Dh ** -0.5)
    qi = jnp.arange(band)[:, None]
    kj = jnp.arange(2 * band)[None, :]
    dist = qi + band - kj
    blk = jnp.arange(n_blk)[:, None, None]
    valid = (dist >= 0) & (dist <= band) & (blk * band - band + kj >= 0)
    s = jnp.where(valid[None, None, :, None], s, -jnp.inf)
    lse = jax.nn.logsumexp(s, axis=-1)
    p = jnp.exp(s - lse[..., None])
    o = jnp.einsum('brnhqk,brnkhd->brnqhd', p.astype(v.dtype), vv)
    o = o.reshape(b, dil, n_blk * band, H, Dh).transpose(0, 2, 1, 3, 4).reshape(b, Tp, H, Dh)[:, :T]
    lse = lse.transpose(0, 2, 4, 1, 3).reshape(b, Tp, H)[:, :T]
    return o, lse


def dilated_attn_sample(q, k, v, kv_cache, window, dil):
    b, S, H, Dh = q.shape
    Wb = kv_cache.shape[1]
    band = window // dil
    k_all = jnp.concatenate([kv_cache[:, :, 0], k], axis=1)
    v_all = jnp.concatenate([kv_cache[:, :, 1], v], axis=1)
    idx = Wb + jnp.arange(S)[:, None] - dil * jnp.arange(band + 1)[None, :]
    valid = idx >= 0
    idx = jnp.maximum(idx, 0)
    kg = k_all[:, idx]
    vg = v_all[:, idx]
    s = jnp.einsum('bshd,bsjhd->bshj', q, kg, preferred_element_type=jnp.float32) * (Dh ** -0.5)
    s = jnp.where(valid[None, :, None, :], s, -jnp.inf)
    lse = jax.nn.logsumexp(s, axis=-1)
    p = jnp.exp(s - lse[..., None])
    o = jnp.einsum('bshj,bsjhd->bshd', p.astype(v.dtype), vg)
    new_kv = jnp.stack([k_all[:, -Wb:], v_all[:, -Wb:]], axis=2)
    return o, lse, new_kv


def mix_ab(h, pos, w_in, ln_g, ln_b, w_s, b_s, w_out, kv_caches):
    b, T, _ = h.shape
    proj = h @ w_in
    u = jax.nn.gelu(proj[..., :A_WIDTH])
    v = layernorm(jax.nn.gelu(proj[..., A_WIDTH:2 * A_WIDTH]), ln_g, ln_b)
    a_out = u * chunk_spatial_gate(v, w_s, b_s)
    qkv = proj[..., 2 * A_WIDTH:].reshape(b, T, B_N_GROUPS, 3, B_HEADS, HEAD_DIM)
    outs, lses, new_kv = [], [], []
    for g, (window, dil) in enumerate(B_CONFIGS):
        q = rope(qkv[:, :, g, 0], pos)
        k = rope(qkv[:, :, g, 1], pos)
        vv = qkv[:, :, g, 2]
        if kv_caches is None:
            o, lse = dilated_attn_prompt(q, k, vv, window, dil)
            keep = min(window, T)
            nkv = jnp.stack([k[:, T - keep:], vv[:, T - keep:]], axis=2)
        else:
            o, lse, nkv = dilated_attn_sample(q, k, vv, kv_caches[g], window, dil)
        outs.append(o)
        lses.append(lse)
        new_kv.append(nkv)
    wts = jax.nn.softmax(jnp.stack(lses, -1), axis=-1)
    b_out = jnp.einsum('bthg,bthgd->bthd', wts.astype(h.dtype), jnp.stack(outs, 3)).reshape(b, T, B_OUT)
    y = jnp.concatenate([a_out, b_out], -1) @ w_out
    return y, new_kv, v


def causal_conv(x, conv_state, w, bias):
    T = x.shape[1]
    ext = jnp.concatenate([conv_state, x], axis=1)
    out = bias
    for j in range(C_CONV):
        out = out + ext[:, j:j + T] * w[j]
    return out, ext[:, T:]


def ssd_scan(x, dt, A, Bm, Cm, h0):
    b, T, NH, P = x.shape
    G, N = Bm.shape[2], Bm.shape[3]
    HPG = NH // G
    Q = min(C_CHUNK, T)
    nc = -(-T // Q)
    Tp = nc * Q

    def pad(t):
        return jnp.pad(t.astype(jnp.float32), [(0, 0), (0, Tp - T)] + [(0, 0)] * (t.ndim - 2))

    xc = pad(x).reshape(b, nc, Q, G, HPG, P)
    dtc = pad(dt).reshape(b, nc, Q, G, HPG)
    Bc = pad(Bm).reshape(b, nc, Q, G, N)
    Cc = pad(Cm).reshape(b, nc, Q, G, N)
    a_cum = jnp.cumsum(dtc * A.astype(jnp.float32).reshape(G, HPG), axis=2)
    xdt = xc * dtc[..., None]
    tri = jnp.tril(jnp.ones((Q, Q), dtype=bool))
    seg = a_cum[:, :, :, None] - a_cum[:, :, None, :]
    decay = jnp.exp(jnp.where(tri[:, :, None, None], seg, -jnp.inf))
    cb = jnp.einsum('bctgn,bcsgn->bctsg', Cc, Bc)
    y_diag = jnp.einsum('bctsgh,bcsghp->bctghp', cb[..., None] * decay, xdt)
    decay_end = jnp.exp(a_cum[:, :, -1:] - a_cum)
    states = jnp.einsum('bcsgn,bcsghp->bcghpn', Bc, xdt * decay_end[..., None])
    chunk_decay = jnp.exp(a_cum[:, :, -1])

    def step(hc, inp):
        dec, st = inp
        return hc * dec[..., None, None] + st, hc

    h_last, h_prev = lax.scan(step, h0.astype(jnp.float32).reshape(b, G, HPG, P, N),
                              (jnp.moveaxis(chunk_decay, 1, 0), jnp.moveaxis(states, 1, 0)))
    h_prev = jnp.moveaxis(h_prev, 0, 1)
    y_off = jnp.einsum('bctgn,bcghpn->bctghp', Cc, h_prev) * jnp.exp(a_cum)[..., None]
    y = (y_diag + y_off).reshape(b, Tp, NH, P)[:, :T]
    return y.astype(x.dtype), h_last.reshape(b, NH, P, N)


def mix_c(h, w_in, conv_w, conv_b, dt_bias, a_log, d_skip, norm_g, w_out, conv_state, ssm_state):
    b, T, _ = h.shape
    proj = h @ w_in
    z = proj[..., :C_D_INNER]
    xbc = proj[..., C_D_INNER:C_D_INNER + C_CONV_DIM]
    dt_raw = proj[..., C_D_INNER + C_CONV_DIM:]
    xbc, new_conv = causal_conv(xbc, conv_state, conv_w, conv_b)
    xbc = jax.nn.silu(xbc)
    xs = xbc[..., :C_D_INNER].reshape(b, T, C_HEADS, C_HEAD_DIM)
    Bm = xbc[..., C_D_INNER:C_D_INNER + C_GROUPS * C_D_STATE].reshape(b, T, C_GROUPS, C_D_STATE)
    Cm = xbc[..., C_D_INNER + C_GROUPS * C_D_STATE:].reshape(b, T, C_GROUPS, C_D_STATE)
    dt = jax.nn.softplus(dt_raw.astype(jnp.float32) + dt_bias.astype(jnp.float32))
    A = -jnp.exp(a_log.astype(jnp.float32))
    y, h_new = ssd_scan(xs, dt, A, Bm, Cm, ssm_state)
    y = y + xs * d_skip[:, None].astype(xs.dtype)
    y = y.reshape(b, T, C_D_INNER) * jax.nn.silu(z)
    y = rmsnorm(y.reshape(b, T, C_GROUPS, C_D_INNER // C_GROUPS),
                norm_g.reshape(C_GROUPS, C_D_INNER // C_GROUPS)).reshape(b, T, C_D_INNER)
    return y @ w_out, new_conv, h_new.astype(h.dtype)


def expert_ffn(xb, wg, wu, wd):
    return (jax.nn.silu(xb @ wg) * (xb @ wu)) @ wd


def hier_moe(x, w_rg, b_rg, w_re, b_re, w_gate, w_up, w_down):
    shp = x.shape
    xf = x.reshape(-1, D_MODEL)
    n = xf.shape[0]
    x32 = xf.astype(jnp.float32)
    pg = jax.nn.softmax(x32 @ w_rg.astype(jnp.float32) + b_rg.astype(jnp.float32), axis=-1)
    p_top, g_top = lax.top_k(pg, 1)
    le = jnp.einsum('nd,dge->nge', x32, w_re.astype(jnp.float32)) + b_re.astype(jnp.float32)
    le_sel = jnp.take_along_axis(le, g_top[:, :, None], axis=1)[:, 0]
    v2, i2 = lax.top_k(le_sel, E_TOPK)
    gate = p_top * jax.nn.softmax(v2, axis=-1)
    eid = g_top * E_PER_GROUP + i2
    n_slots = n * E_TOPK
    blk = max(8, min(MOE_BLOCK, n_slots // N_EXPERTS))
    n_blocks = n_slots // blk + N_EXPERTS
    flat_e = eid.reshape(-1)
    order = jnp.argsort(flat_e)
    e_sorted = flat_e[order]
    tok = order // E_TOPK
    counts = jnp.bincount(flat_e, length=N_EXPERTS)
    padded = (counts + blk - 1) // blk * blk
    pad_end = jnp.cumsum(padded)
    pad_start = pad_end - padded
    start = jnp.cumsum(counts) - counts
    dest = pad_start[e_sorted] + jnp.arange(n_slots) - start[e_sorted]
    x_rows = jnp.zeros((n_blocks * blk, D_MODEL), x.dtype).at[dest].set(xf[tok])
    blk_expert = jnp.minimum(jnp.searchsorted(pad_end, jnp.arange(n_blocks) * blk, side='right'), N_EXPERTS - 1)
    y_blocks = lax.map(lambda a: expert_ffn(a[0], w_gate[a[1]], w_up[a[1]], w_down[a[1]]),
                       (x_rows.reshape(n_blocks, blk, D_MODEL), blk_expert))
    y_slot = y_blocks.reshape(-1, D_MODEL)[dest] * gate.reshape(-1)[order][:, None].astype(x.dtype)
    out = jnp.zeros_like(xf).at[tok].add(y_slot)
    return out.reshape(shp)


def setup_inputs(seed: int = 0) -> dict:
    key = jax.random.key(seed)
    ks = iter(jax.random.split(key, 40))

    def nrm(shape, scale):
        return jax.random.normal(next(ks), shape, jnp.float32) * scale

    def gain(shape):
        return 1.0 + nrm(shape, 0.02)

    def kv_shape(w):
        return (N_AB_LAYERS, DEC_BATCH, min(w, PAST_LEN), 2, B_HEADS, HEAD_DIM)

    dt0 = jnp.exp(jax.random.uniform(next(ks), (N_C_LAYERS, C_HEADS), jnp.float32, math.log(1e-3), math.log(1e-1)))
    a_log = jnp.log(jax.random.uniform(next(ks), (N_C_LAYERS, C_HEADS), jnp.float32, 1.0, 16.0))
    return {
        'x_prompt': nrm((BATCH, SEQ, D_MODEL), 1.0),
        'x_sample': nrm((DEC_BATCH, DEC_SEQ, D_MODEL), 1.0),
        'cache_kv_w128': nrm(kv_shape(B_CONFIGS[0][0]), 1.0),
        'cache_kv_w512': nrm(kv_shape(B_CONFIGS[1][0]), 1.0),
        'cache_kv_w2048': nrm(kv_shape(B_CONFIGS[2][0]), 1.0),
        'state_conv': nrm((N_C_LAYERS, DEC_BATCH, C_CONV - 1, C_CONV_DIM), 1.0),
        'state_ssm': nrm((N_C_LAYERS, DEC_BATCH, C_HEADS, C_HEAD_DIM, C_D_STATE), 0.3),
        'norm_mix': gain((DEPTH, D_MODEL)),
        'norm_ffn': gain((DEPTH, D_MODEL)),
        'norm_final': gain((D_MODEL,)),
        'w_in_ab': nrm((N_AB_LAYERS, D_MODEL, AB_IN), D_MODEL ** -0.5),
        'a_ln_g': gain((N_AB_LAYERS, A_WIDTH)),
        'a_ln_b': nrm((N_AB_LAYERS, A_WIDTH), 0.02),
        'a_w_s': nrm((N_AB_LAYERS, A_GROUPS, A_CHUNK, A_CHUNK), A_CHUNK ** -0.5),
        'a_b_s': 1.0 + nrm((N_AB_LAYERS, A_GROUPS, A_CHUNK), 0.1),
        'w_out_ab': nrm((N_AB_LAYERS, AB_OUT, D_MODEL), AB_OUT ** -0.5),
        'w_in_c': nrm((N_C_LAYERS, D_MODEL, C_IN), D_MODEL ** -0.5),
        'c_conv_w': nrm((N_C_LAYERS, C_CONV, C_CONV_DIM), C_CONV ** -0.5),
        'c_conv_b': nrm((N_C_LAYERS, C_CONV_DIM), 0.02),
        'c_dt_bias': dt0 + jnp.log(-jnp.expm1(-dt0)),
        'c_a_log': a_log,
        'c_d': 1.0 + nrm((N_C_LAYERS, C_HEADS), 0.1),
        'c_norm_g': gain((N_C_LAYERS, C_D_INNER)),
        'w_out_c': nrm((N_C_LAYERS, C_D_INNER, D_MODEL), C_D_INNER ** -0.5),
        'w_router_g': nrm((DEPTH, D_MODEL, E_GROUPS), D_MODEL ** -0.5),
        'b_router_g': nrm((DEPTH, E_GROUPS), 0.01),
        'w_router_e': nrm((DEPTH, D_MODEL, E_GROUPS, E_PER_GROUP), D_MODEL ** -0.5),
        'b_router_e': nrm((DEPTH, E_GROUPS, E_PER_GROUP), 0.01),
        'w_exp_gate': nrm((DEPTH, N_EXPERTS, D_MODEL, E_FF), D_MODEL ** -0.5),
        'w_exp_up': nrm((DEPTH, N_EXPERTS, D_MODEL, E_FF), D_MODEL ** -0.5),
        'w_exp_down': nrm((DEPTH, N_EXPERTS, E_FF, D_MODEL), E_FF ** -0.5),
    }


def reference(x_prompt, x_sample, cache_kv_w128, cache_kv_w512, cache_kv_w2048, state_conv, state_ssm,
              norm_mix, norm_ffn, norm_final, w_in_ab, a_ln_g, a_ln_b, a_w_s, a_b_s, w_out_ab,
              w_in_c, c_conv_w, c_conv_b, c_dt_bias, c_a_log, c_d, c_norm_g, w_out_c,
              w_router_g, b_router_g, w_router_e, b_router_e, w_exp_gate, w_exp_up, w_exp_down):
    kv_caches = (cache_kv_w128, cache_kv_w512, cache_kv_w2048)
    T_p = x_prompt.shape[1]
    T_s = x_sample.shape[1]
    pos_p = jnp.arange(T_p, dtype=jnp.int32)
    pos_s = PAST_LEN + jnp.arange(T_s, dtype=jnp.int32)
    xp, xs = x_prompt, x_sample
    kv_p = [[] for _ in B_CONFIGS]
    kv_s = [[] for _ in B_CONFIGS]
    chunk_v_s, conv_p, conv_s, ssm_p, ssm_s = [], [], [], [], []
    for l in range(DEPTH):
        i = l // 2
        if l % 2 == 0:
            ab = (w_in_ab[i], a_ln_g[i], a_ln_b[i], a_w_s[i], a_b_s[i], w_out_ab[i])
            yp, nkv_p, _ = mix_ab(rmsnorm(xp, norm_mix[l]), pos_p, *ab, None)
            ys, nkv_s, v_s = mix_ab(rmsnorm(xs, norm_mix[l]), pos_s, *ab, tuple(c[i] for c in kv_caches))
            for g in range(B_N_GROUPS):
                kv_p[g].append(nkv_p[g])
                kv_s[g].append(nkv_s[g])
            chunk_v_s.append(v_s[:, T_s - T_s % A_CHUNK:])
        else:
            cw = (w_in_c[i], c_conv_w[i], c_conv_b[i], c_dt_bias[i], c_a_log[i], c_d[i], c_norm_g[i], w_out_c[i])
            zc = jnp.zeros((xp.shape[0], C_CONV - 1, C_CONV_DIM), xp.dtype)
            zs = jnp.zeros((xp.shape[0], C_HEADS, C_HEAD_DIM, C_D_STATE), xp.dtype)
            yp, ncp, nsp = mix_c(rmsnorm(xp, norm_mix[l]), *cw, zc, zs)
            ys, ncs, nss = mix_c(rmsnorm(xs, norm_mix[l]), *cw, state_conv[i], state_ssm[i])
            conv_p.append(ncp)
            conv_s.append(ncs)
            ssm_p.append(nsp)
            ssm_s.append(nss)
        xp = xp + yp
        xs = xs + ys
        moe = (w_router_g[l], b_router_g[l], w_router_e[l], b_router_e[l], w_exp_gate[l], w_exp_up[l], w_exp_down[l])
        xp = xp + hier_moe(rmsnorm(xp, norm_ffn[l]), *moe)
        xs = xs + hier_moe(rmsnorm(xs, norm_ffn[l]), *moe)
    y_prompt = rmsnorm(xp, norm_final)
    y_sample = rmsnorm(xs, norm_final)
    return (y_prompt, y_sample,
            jnp.stack(kv_p[0]), jnp.stack(kv_s[0]),
            jnp.stack(kv_p[1]), jnp.stack(kv_s[1]),
            jnp.stack(kv_p[2]), jnp.stack(kv_s[2]),
            jnp.stack(chunk_v_s),
            jnp.stack(conv_p), jnp.stack(conv_s),
            jnp.stack(ssm_p), jnp.stack(ssm_s))
```

```python
import functools
import math

import jax
import jax.numpy as jnp
from jax import lax
from jax.experimental import pallas as pl
from jax.experimental.pallas import tpu as pltpu

F32 = jnp.float32
BF16 = jnp.bfloat16

D_MODEL = 2048
PAST_LEN = 16384
EPS = 1e-6
A_WIDTH = 1024
A_GROUPS = 8
A_GW = 128
A_CHUNK = 128
HEAD_DIM = 128
B_HEADS = 4
B_CONFIGS = ((128, 1), (512, 4), (2048, 16))
B_BAND = 128
B_HW = B_HEADS * HEAD_DIM
AB_IN = 2 * A_WIDTH + 3 * len(B_CONFIGS) * B_HW
ROPE_THETA = 10000.0
C_D_INNER = 4096
C_HEAD_DIM = 64
C_HEADS = 64
C_GROUPS = 8
C_D_STATE = 128
C_CONV = 4
C_CONV_DIM = C_D_INNER + 2 * C_GROUPS * C_D_STATE
C_GW = C_D_INNER // C_GROUPS
E_GROUPS = 8
E_PER_GROUP = 8
N_EXPERTS = 64
E_FF = 512
MOE_ROWS = 256

LANES = 128
VMEM_LIMIT = 56 * 1024 * 1024


def _cparams(sem):
    return pltpu.CompilerParams(dimension_semantics=sem, vmem_limit_bytes=VMEM_LIMIT)


def _split3(x):
    hi = x.astype(BF16)
    r1 = x - hi.astype(F32)
    mid = r1.astype(BF16)
    lo = (r1 - mid.astype(F32)).astype(BF16)
    return hi, mid, lo


def _dot(a, b):
    return jnp.dot(a, b, preferred_element_type=F32)


def _dot_nt(a, b):
    return lax.dot_general(a, b, (((1,), (1,)), ((), ())), preferred_element_type=F32)


def _rmsnorm_rows(x, g):
    ms = jnp.mean(x * x, axis=-1, keepdims=True)
    return x * lax.rsqrt(ms + EPS) * g


def _inproj_kernel(x_ref, g_ref, w_ref, *rest, rope):
    o_ref, h_scr = rest[-2:]
    j = pl.program_id(1)

    @pl.when(j == 0)
    def _():
        h_scr[...] = _rmsnorm_rows(x_ref[...], g_ref[...]).astype(BF16)

    acc = _dot(h_scr[...], w_ref[...])
    if not rope:
        o_ref[...] = acc
        return
    cos_ref, sin_ref = rest[:2]
    is_qk = jnp.logical_and(j >= 4, (j - 4) % 3 != 2)

    @pl.when(is_qk)
    def _():
        cos = cos_ref[...]
        sin = sin_ref[...]
        for h in range(B_HEADS):
            sl = slice(h * HEAD_DIM, (h + 1) * HEAD_DIM)
            a = acc[:, sl]
            o_ref[:, sl] = a * cos + pltpu.roll(a, HEAD_DIM // 2, 1) * sin

    @pl.when(jnp.logical_not(is_qk))
    def _():
        o_ref[...] = acc


def _inproj(x, g, w_bf16, tabs=None, *, tm):
    m, _ = x.shape
    n = w_bf16.shape[1]
    tn = B_HW if n % B_HW == 0 else n
    in_specs = [
        pl.BlockSpec((tm, D_MODEL), lambda i, j: (i, 0)),
        pl.BlockSpec((1, D_MODEL), lambda i, j: (0, 0)),
        pl.BlockSpec((D_MODEL, tn), lambda i, j: (0, j)),
    ]
    args = [x, g.reshape(1, D_MODEL), w_bf16]
    if tabs is not None:
        n_pos_tiles = tabs[0].shape[0] // tm
        in_specs += [pl.BlockSpec((tm, HEAD_DIM), lambda i, j: (i % n_pos_tiles, 0))] * 2
        args += list(tabs)
    return pl.pallas_call(
        functools.partial(_inproj_kernel, rope=tabs is not None),
        grid=(m // tm, n // tn),
        in_specs=in_specs,
        out_specs=pl.BlockSpec((tm, tn), lambda i, j: (i, j)),
        out_shape=jax.ShapeDtypeStruct((m, n), F32),
        scratch_shapes=[pltpu.VMEM((tm, D_MODEL), BF16)],
        compiler_params=_cparams(("parallel", "arbitrary")),
        name="inproj",
    )(*args)


def _gmlp_kernel(p_ref, lng_ref, lnb_ref, ws_ref, bs_ref, a_ref, *v_ref, chunks):
    row = lax.broadcasted_iota(jnp.int32, (A_CHUNK, A_CHUNK), 0)
    col = lax.broadcasted_iota(jnp.int32, (A_CHUNK, A_CHUNK), 1)
    causal = col <= row
    for c in range(chunks):
        rows = slice(c * A_CHUNK, (c + 1) * A_CHUNK)
        u = jax.nn.gelu(p_ref[rows, 0:A_WIDTH], approximate=True)
        vg = jax.nn.gelu(p_ref[rows, A_WIDTH:2 * A_WIDTH], approximate=True)
        vc = vg - jnp.mean(vg, axis=-1, keepdims=True)
        v = vc * lax.rsqrt(jnp.mean(vc * vc, axis=-1, keepdims=True) + EPS)
        v = v * lng_ref[...] + lnb_ref[...]
        if v_ref:
            v_ref[0][rows, :] = v
        vb = v.astype(BF16)
        for g in range(A_GROUPS):
            sl = slice(g * A_GW, (g + 1) * A_GW)
            w = jnp.where(causal, ws_ref[g], 0.0).astype(BF16)
            mixed = _dot(w, vb[:, sl]) + bs_ref[:, sl]
            a_ref[rows, sl] = (u[:, sl] * mixed).astype(BF16)


def _gmlp(proj, ln_g, ln_b, w_s, b_s, *, n_rows, chunks, emit_v):
    rows = chunks * A_CHUNK
    bsb = jnp.repeat(jnp.transpose(b_s), A_GW, axis=1)
    out_shape = [jax.ShapeDtypeStruct((n_rows, A_WIDTH), BF16)]
    out_specs = [pl.BlockSpec((rows, A_WIDTH), lambda i: (i, 0))]
    if emit_v:
        out_shape.append(jax.ShapeDtypeStruct((n_rows, A_WIDTH), F32))
        out_specs.append(pl.BlockSpec((rows, A_WIDTH), lambda i: (i, 0)))
    return pl.pallas_call(
        functools.partial(_gmlp_kernel, chunks=chunks),
        grid=(n_rows // rows,),
        in_specs=[
            pl.BlockSpec((rows, 2 * A_WIDTH), lambda i: (i, 0)),
            pl.BlockSpec((1, A_WIDTH), lambda i: (0, 0)),
            pl.BlockSpec((1, A_WIDTH), lambda i: (0, 0)),
            pl.BlockSpec((A_GROUPS, A_CHUNK, A_CHUNK), lambda i: (0, 0, 0)),
            pl.BlockSpec((A_CHUNK, A_WIDTH), lambda i: (0, 0)),
        ],
        out_specs=out_specs,
        out_shape=out_shape,
        compiler_params=_cparams(("parallel",)),
        name="gmlp",
    )(proj, ln_g.reshape(1, A_WIDTH), ln_b.reshape(1, A_WIDTH), w_s, bsb)


def _attn_prompt_kernel(q_ref, kp_ref, kc_ref, vp_ref, vc_ref, o_ref, lse_ref, *, dil, heads):
    n = pl.program_id(2)
    qi = lax.broadcasted_iota(jnp.int32, (B_BAND, B_BAND), 0)
    kj = lax.broadcasted_iota(jnp.int32, (B_BAND, B_BAND), 1)
    prev_ok = jnp.logical_and(kj >= qi, n > 0)
    cur_ok = kj <= qi
    scale = HEAD_DIM ** -0.5
    for r in range(dil):
        rows = pl.ds(r, B_BAND, stride=dil) if dil > 1 else slice(None)
        for h in range(heads):
            sl = slice(h * HEAD_DIM, (h + 1) * HEAD_DIM)
            q = q_ref[rows, sl].astype(BF16)
            sp = _dot_nt(q, kp_ref[rows, sl].astype(BF16)) * scale
            sc = _dot_nt(q, kc_ref[rows, sl].astype(BF16)) * scale
            sp = jnp.where(prev_ok, sp, -jnp.inf)
            sc = jnp.where(cur_ok, sc, -jnp.inf)
            m = jnp.maximum(jnp.max(sp, axis=-1, keepdims=True), jnp.max(sc, axis=-1, keepdims=True))
            pp = jnp.exp(sp - m)
            pc = jnp.exp(sc - m)
            l = jnp.sum(pp, axis=-1, keepdims=True) + jnp.sum(pc, axis=-1, keepdims=True)
            o = (_dot(pp.astype(BF16), vp_ref[rows, sl].astype(BF16))
                 + _dot(pc.astype(BF16), vc_ref[rows, sl].astype(BF16)))
            o_ref[rows, sl] = o / l
            lse_ref[rows, sl] = jnp.broadcast_to(m + jnp.log(l), (B_BAND, HEAD_DIM))


def _attn_prompt(proj, group, *, batch, seq):
    dil = B_CONFIGS[group][1]
    rows = B_BAND * dil
    nblk = seq // rows
    heads = B_HEADS if dil == 1 else 1
    width = heads * HEAD_DIM
    n_col = B_HW // width
    qcol = (4 + 3 * group) * n_col

    def spec(off, prev):
        def imap(b, h, n):
            nn = jnp.maximum(n - 1, 0) if prev else n
            return (b * nblk + nn, qcol + off * n_col + h)
        return pl.BlockSpec((rows, width), imap)

    out_spec = pl.BlockSpec((rows, width), lambda b, h, n: (b * nblk + n, h))
    return pl.pallas_call(
        functools.partial(_attn_prompt_kernel, dil=dil, heads=heads),
        grid=(batch, n_col, nblk),
        in_specs=[spec(0, False), spec(1, True), spec(1, False), spec(2, True), spec(2, False)],
        out_specs=[out_spec, out_spec],
        out_shape=[jax.ShapeDtypeStruct((batch * seq, B_HW), F32)] * 2,
        compiler_params=_cparams(("parallel", "parallel", "arbitrary")),
        name=f"attn_prompt_g{group}",
    )(proj, proj, proj, proj, proj)


def _merge_kernel(o0, o1, o2, l0, l1, l2, b_ref):
    m = jnp.maximum(jnp.maximum(l0[...], l1[...]), l2[...])
    e0 = jnp.exp(l0[...] - m)
    e1 = jnp.exp(l1[...] - m)
    e2 = jnp.exp(l2[...] - m)
    den = e0 + e1 + e2
    b_ref[...] = ((e0 / den) * o0[...] + (e1 / den) * o1[...] + (e2 / den) * o2[...]).astype(BF16)


def _merge(outs, lses, *, tm):
    m = outs[0].shape[0]
    spec = pl.BlockSpec((tm, B_HW), lambda i: (i, 0))
    return pl.pallas_call(
        _merge_kernel,
        grid=(m // tm,),
        in_specs=[spec] * 6,
        out_specs=spec,
        out_shape=jax.ShapeDtypeStruct((m, B_HW), BF16),
        compiler_params=_cparams(("parallel",)),
        name="attn_merge",
    )(*outs, *lses)


SAMPLE_ROWS = 16


def _attn_sample_kernel(qkv_ref, c0_ref, c1_ref, c2_ref, b_ref, *, n_real):
    caches = (c0_ref, c1_ref, c2_ref)
    row_w = 2 * B_HW
    scale = HEAD_DIM ** -0.5
    qi_c = lax.broadcasted_iota(jnp.int32, (SAMPLE_ROWS, B_BAND), 0)
    ka_c = lax.broadcasted_iota(jnp.int32, (SAMPLE_ROWS, B_BAND), 1)
    qi_n = lax.broadcasted_iota(jnp.int32, (SAMPLE_ROWS, SAMPLE_ROWS), 0)
    km_n = lax.broadcasted_iota(jnp.int32, (SAMPLE_ROWS, SAMPLE_ROWS), 1)

    def valid(diff, dil):
        shift = dil.bit_length() - 1
        ok = jnp.logical_and(diff >= 0, (diff & (dil - 1)) == 0)
        return jnp.logical_and(ok, (diff >> shift) <= B_BAND)

    for h in range(B_HEADS):
        hs = slice(h * HEAD_DIM, (h + 1) * HEAD_DIM)
        outs, lses = [], []
        for g, (window, dil) in enumerate(B_CONFIGS):
            base = g * 3 * B_HW
            q = qkv_ref[0, :, base + h * HEAD_DIM: base + (h + 1) * HEAD_DIM].astype(BF16)
            kn = qkv_ref[0, :, base + B_HW + h * HEAD_DIM: base + B_HW + (h + 1) * HEAD_DIM].astype(BF16)
            vn = qkv_ref[0, :, base + 2 * B_HW + h * HEAD_DIM: base + 2 * B_HW + (h + 1) * HEAD_DIM].astype(BF16)
            n_res = min(dil, n_real)
            scores, values = [], []
            for r in range(n_res):
                kc = caches[g][0, :, r * row_w + h * HEAD_DIM: r * row_w + (h + 1) * HEAD_DIM].astype(BF16)
                vc = caches[g][0, :, r * row_w + B_HW + h * HEAD_DIM: r * row_w + B_HW + (h + 1) * HEAD_DIM].astype(BF16)
                s = _dot_nt(q, kc) * scale
                diff = qi_c + window - ka_c * dil - r
                scores.append(jnp.where(valid(diff, dil), s, -jnp.inf))
                values.append(vc)
            s = _dot_nt(q, kn) * scale
            scores.append(jnp.where(valid(qi_n - km_n, dil), s, -jnp.inf))
            values.append(vn)
            m = functools.reduce(jnp.maximum, [jnp.max(s, axis=-1, keepdims=True) for s in scores])
            ps = [jnp.exp(s - m) for s in scores]
            l = functools.reduce(jnp.add, [jnp.sum(p, axis=-1, keepdims=True) for p in ps])
            o = functools.reduce(jnp.add, [_dot(p.astype(BF16), v) for p, v in zip(ps, values)])
            outs.append(o / l)
            lses.append(m + jnp.log(l))
        m = functools.reduce(jnp.maximum, lses)
        es = [jnp.exp(l - m) for l in lses]
        den = functools.reduce(jnp.add, es)
        b_ref[0, :, hs] = functools.reduce(jnp.add, [(e / den) * o for e, o in zip(es, outs)]).astype(BF16)


def _attn_sample(qkv, caches, *, n_real):
    b = qkv.shape[0]
    views, specs = [], []
    for (window, dil), c in zip(B_CONFIGS, caches):
        assert c.shape[1] == window and window == dil * B_BAND
        n_res = min(dil, n_real)
        views.append(c.reshape(b, B_BAND, dil * 2 * B_HW))
        specs.append(pl.BlockSpec((1, B_BAND, n_res * 2 * B_HW), lambda i: (i, 0, 0)))
    return pl.pallas_call(
        functools.partial(_attn_sample_kernel, n_real=n_real),
        grid=(b,),
        in_specs=[pl.BlockSpec((1, SAMPLE_ROWS, qkv.shape[2]), lambda i: (i, 0, 0))] + specs,
        out_specs=pl.BlockSpec((1, SAMPLE_ROWS, B_HW), lambda i: (i, 0, 0)),
        out_shape=jax.ShapeDtypeStruct((b, SAMPLE_ROWS, B_HW), BF16),
        compiler_params=_cparams(("parallel",)),
        name="attn_sample",
    )(qkv, *views)


def _outproj_kernel(*refs, n_parts):
    a_refs = refs[:n_parts]
    w_refs = refs[n_parts:2 * n_parts]
    x_ref, o_ref = refs[2 * n_parts], refs[2 * n_parts + 1]
    acc = x_ref[...]
    for a, w in zip(a_refs, w_refs):
        acc = acc + _dot(a[...], w[...])
    o_ref[...] = acc


def _outproj(parts, weights, x, *, tm, tn):
    m, n = x.shape
    in_specs = [pl.BlockSpec((tm, a.shape[1]), lambda j, i: (i, 0)) for a in parts]
    in_specs += [pl.BlockSpec((w.shape[0], tn), lambda j, i: (0, j)) for w in weights]
    in_specs.append(pl.BlockSpec((tm, tn), lambda j, i: (i, j)))
    return pl.pallas_call(
        functools.partial(_outproj_kernel, n_parts=len(parts)),
        grid=(n // tn, m // tm),
        in_specs=in_specs,
        out_specs=pl.BlockSpec((tm, tn), lambda j, i: (i, j)),
        out_shape=jax.ShapeDtypeStruct((m, n), F32),
        compiler_params=_cparams(("parallel", "parallel")),
        name="outproj",
    )(*parts, *weights, x)


CONV_PAD = 8


def _ssd_kernel(z_ref, xa_ref, xb_ref, bc_ref, dtr_ref, dtrt_ref, cst_ref, cw_ref, cb_ref,
                dtb_ref, dtbc_ref, alog_ref, alogc_ref, dsk_ref, ng_ref, e_ref, h0_ref,
                y_ref, h_ref, ext_scr, act_scr, *, q_len, t_valid):
    c = pl.program_id(1)

    @pl.when(c == 0)
    def _():
        ext_scr[0:CONV_PAD, :] = cst_ref[0]
        h_ref[...] = h0_ref[...]

    half = C_CONV_DIM // 3
    ext_scr[CONV_PAD:CONV_PAD + q_len, 0:half] = xa_ref[...]
    ext_scr[CONV_PAD:CONV_PAD + q_len, half:2 * half] = xb_ref[...]
    ext_scr[CONV_PAD:CONV_PAD + q_len, 2 * half:3 * half] = bc_ref[...]
    for s in range(C_CONV_DIM // C_GW):
        sl = slice(s * C_GW, (s + 1) * C_GW)
        acc = cb_ref[:, sl]
        for j in range(C_CONV):
            lo = CONV_PAD - (C_CONV - 1) + j
            acc = acc + ext_scr[lo:lo + q_len, sl] * cw_ref[j:j + 1, sl]
        act_scr[:, sl] = acc * jax.nn.sigmoid(acc)
    ext_scr[0:CONV_PAD, :] = ext_scr[q_len:q_len + CONV_PAD, :]

    ti = lax.broadcasted_iota(jnp.int32, (q_len, q_len), 0)
    si = lax.broadcasted_iota(jnp.int32, (q_len, q_len), 1)
    tri = si <= ti
    tri_b = jnp.where(tri, 1.0, 0.0).astype(BF16)
    tri_tb = jnp.where(ti <= si, 1.0, 0.0).astype(BF16)

    dt = jax.nn.softplus(dtr_ref[...] + dtb_ref[...])
    dtt = jax.nn.softplus(dtrt_ref[...] + dtbc_ref[...])
    if t_valid < q_len:
        dt = jnp.where(lax.broadcasted_iota(jnp.int32, dt.shape, 0) < t_valid, dt, 0.0)
        dtt = jnp.where(lax.broadcasted_iota(jnp.int32, dtt.shape, 1) < t_valid, dtt, 0.0)
    da = dt * (-jnp.exp(alog_ref[...]))
    dat = dtt * (-jnp.exp(alogc_ref[...]))
    a_cum = functools.reduce(jnp.add, [_dot(tri_b, p) for p in reversed(_split3(da))])
    a_cumt = functools.reduce(jnp.add, [_dot(p, tri_tb) for p in reversed(_split3(dat))])
    dt_parts = _split3(dt)
    ac_parts = _split3(a_cum)

    lane_lo = lax.broadcasted_iota(jnp.int32, (q_len, LANES), 1) < C_HEAD_DIM

    for g in range(C_GROUPS):
        gs = slice(g * C_GW, (g + 1) * C_GW)
        e_g = e_ref[:, gs]
        dt_x = functools.reduce(jnp.add, [_dot(p, e_g) for p in reversed(dt_parts)])
        ac_x = functools.reduce(jnp.add, [_dot(p, e_g) for p in reversed(ac_parts)])
        xs = act_scr[:, gs]
        bm = act_scr[:, C_D_INNER + g * C_D_STATE: C_D_INNER + (g + 1) * C_D_STATE]
        cm = act_scr[:, C_D_INNER + C_GROUPS * C_D_STATE + g * C_D_STATE:
                     C_D_INNER + C_GROUPS * C_D_STATE + (g + 1) * C_D_STATE]
        bm_b = bm.astype(BF16)
        cm_b = cm.astype(BF16)
        xdt = xs * dt_x
        a_last = ac_x[q_len - 1:q_len, :]
        xdt_b = xdt.astype(BF16)
        xdt_end_b = (xdt * jnp.exp(a_last - ac_x)).astype(BF16)
        cbm = _dot_nt(cm_b, bm_b)

        h_prev = h_ref[0, g]
        y = _dot(cm_b, h_prev.astype(BF16)) * jnp.exp(ac_x)
        st = _dot(jnp.transpose(bm).astype(BF16), xdt_end_b)
        h_ref[0, g] = h_prev * jnp.exp(a_last) + st

        pairs = []
        for hp in range(C_GW // LANES):
            ms = []
            for hh in (2 * hp, 2 * hp + 1):
                hd = g * (C_GW // C_HEAD_DIM) + hh
                seg = a_cum[:, hd:hd + 1] - a_cumt[hd:hd + 1, :]
                decay = jnp.exp(jnp.where(tri, seg, -jnp.inf))
                ms.append((cbm * decay).astype(BF16))
            slab = xdt_b[:, hp * LANES:(hp + 1) * LANES]
            zero = jnp.zeros_like(slab)
            rhs = jnp.concatenate([jnp.where(lane_lo, slab, zero), jnp.where(lane_lo, zero, slab)], axis=0)
            pairs.append(_dot(jnp.concatenate(ms, axis=1), rhs))
        y = y + jnp.concatenate(pairs, axis=1)
        y = y + xs * dsk_ref[:, gs]
        zz = z_ref[:, gs]
        y = y * (zz * jax.nn.sigmoid(zz))
        y = y * lax.rsqrt(jnp.mean(y * y, axis=-1, keepdims=True) + EPS) * ng_ref[:, gs]
        y_ref[:, gs] = y.astype(BF16)


def _ssd(proj, dt_raw, conv_state, h0t, conv_w, conv_b, dt_bias, a_log, d_skip, norm_g,
         *, batch, n_chunks, q_len, t_valid):
    rows = batch * n_chunks * q_len
    dtrt = dt_raw.reshape(batch * n_chunks, q_len, LANES).transpose(0, 2, 1).reshape(-1, q_len)
    pad = LANES - C_HEADS
    col = lambda v: jnp.pad(v, (0, pad)).reshape(LANES, 1)
    row = lambda v: jnp.pad(v, (0, pad)).reshape(1, LANES)
    expand = jnp.repeat(jnp.eye(LANES, C_HEADS, dtype=BF16), C_HEAD_DIM, axis=1)
    cw = jnp.pad(conv_w, ((0, CONV_PAD - C_CONV), (0, 0)))
    xcol = C_D_INNER // (C_CONV_DIM // 3)
    const = lambda shape: pl.BlockSpec(shape, lambda b, c: (0,) * len(shape))
    blk = lambda width, j: pl.BlockSpec((q_len, width), lambda b, c: (b * n_chunks + c, j))
    y, h = pl.pallas_call(
        functools.partial(_ssd_kernel, q_len=q_len, t_valid=t_valid),
        grid=(batch, n_chunks),
        in_specs=[
            blk(C_D_INNER, 0), blk(C_CONV_DIM // 3, xcol), blk(C_CONV_DIM // 3, xcol + 1),
            blk(C_CONV_DIM // 3, xcol + 2), blk(LANES, 0),
            pl.BlockSpec((LANES, q_len), lambda b, c: (b * n_chunks + c, 0)),
            pl.BlockSpec((1, CONV_PAD, C_CONV_DIM), lambda b, c: (b, 0, 0)),
            const((CONV_PAD, C_CONV_DIM)), const((1, C_CONV_DIM)),
            const((1, LANES)), const((LANES, 1)), const((1, LANES)), const((LANES, 1)),
            const((1, C_D_INNER)), const((1, C_D_INNER)), const((LANES, C_D_INNER)),
            pl.BlockSpec((1, C_GROUPS, C_D_STATE, C_GW), lambda b, c: (b, 0, 0, 0)),
        ],
        out_specs=[
            pl.BlockSpec((q_len, C_D_INNER), lambda b, c: (b * n_chunks + c, 0)),
            pl.BlockSpec((1, C_GROUPS, C_D_STATE, C_GW), lambda b, c: (b, 0, 0, 0)),
        ],
        out_shape=[
            jax.ShapeDtypeStruct((rows, C_D_INNER), BF16),
            jax.ShapeDtypeStruct((batch, C_GROUPS, C_D_STATE, C_GW), F32),
        ],
        scratch_shapes=[
            pltpu.VMEM((q_len + CONV_PAD, C_CONV_DIM), F32),
            pltpu.VMEM((q_len, C_CONV_DIM), F32),
        ],
        compiler_params=_cparams(("parallel", "arbitrary")),
        name="ssd",
    )(proj, proj, proj, proj, dt_raw, dtrt, conv_state, cw, conv_b.reshape(1, -1),
      row(dt_bias), col(dt_bias), row(a_log), col(a_log),
      jnp.repeat(d_skip, C_HEAD_DIM).reshape(1, -1), norm_g.reshape(1, -1), expand, h0t)
    return y, h


def _state_to_t(h):
    b = h.shape[0]
    return h.reshape(b, C_GROUPS, C_HEADS // C_GROUPS, C_HEAD_DIM, C_D_STATE).transpose(0, 1, 4, 2, 3).reshape(
        b, C_GROUPS, C_D_STATE, C_GW)


def _state_from_t(ht):
    b = ht.shape[0]
    return ht.reshape(b, C_GROUPS, C_D_STATE, C_HEADS // C_GROUPS, C_HEAD_DIM).transpose(0, 1, 3, 4, 2).reshape(
        b, C_HEADS, C_HEAD_DIM, C_D_STATE)


def _pack_bf16_pairs(h):
    k = h.shape[1] // 2
    bits = pltpu.bitcast(h.astype(BF16).astype(F32), jnp.uint32)
    packed = (bits[:, :k] & jnp.uint32(0xFFFF0000)) | (bits[:, k:] >> 16)
    return pltpu.bitcast(packed, jnp.int32)


def _unpack_bf16_pairs(p):
    bits = pltpu.bitcast(p, jnp.uint32)
    hi = pltpu.bitcast(bits & jnp.uint32(0xFFFF0000), F32)
    lo = pltpu.bitcast(bits << 16, F32)
    return jnp.concatenate([hi, lo], axis=1).astype(BF16)


def _router_kernel(x_ref, g_ref, w_ref, b_ref, base_ref, h_ref, mi_ref, mf_ref, cnt_ref):
    tm = x_ref.shape[0]

    @pl.when(pl.program_id(0) == 0)
    def _():
        cnt_ref[...] = base_ref[...]

    h = _rmsnorm_rows(x_ref[...], g_ref[...])
    h_ref[...] = _pack_bf16_pairs(h)
    h_hi, h_lo, _ = _split3(h)
    w_hi, w_lo, _ = _split3(w_ref[...])
    logits = (_dot(h_lo, w_hi) + _dot(h_hi, w_lo)) + _dot(h_hi, w_hi) + b_ref[...]

    lane = lax.broadcasted_iota(jnp.int32, (tm, LANES), 1).astype(F32)
    big = float(4 * LANES)
    lg = jnp.where(lane < E_GROUPS, logits, -jnp.inf)
    mg = jnp.max(lg, axis=-1, keepdims=True)
    p_top = 1.0 / jnp.sum(jnp.exp(lg - mg), axis=-1, keepdims=True)
    g_top = jnp.min(jnp.where(lg == mg, lane, big), axis=-1, keepdims=True)
    lo = E_GROUPS + g_top * E_PER_GROUP
    le = jnp.where(jnp.logical_and(lane >= lo, lane < lo + E_PER_GROUP), logits, -jnp.inf)
    v1 = jnp.max(le, axis=-1, keepdims=True)
    i1 = jnp.min(jnp.where(le == v1, lane, big), axis=-1, keepdims=True)
    le2 = jnp.where(lane == i1, -jnp.inf, le)
    v2 = jnp.max(le2, axis=-1, keepdims=True)
    i2 = jnp.min(jnp.where(le2 == v2, lane, big), axis=-1, keepdims=True)
    e21 = jnp.exp(v2 - v1)
    gate1 = p_top / (1.0 + e21)
    gate2 = p_top * e21 / (1.0 + e21)
    eid1 = i1 - E_GROUPS
    eid2 = i2 - E_GROUPS

    oh1 = lane == eid1
    oh2 = lane == eid2
    oh = jnp.where(jnp.logical_or(oh1, oh2), 1.0, 0.0)
    ri = lax.broadcasted_iota(jnp.int32, (tm, tm), 0)
    ci = lax.broadcasted_iota(jnp.int32, (tm, tm), 1)
    before = jnp.where(ci < ri, 1.0, 0.0).astype(BF16)
    pos = _dot(before, oh.astype(BF16)) + cnt_ref[...]
    rank1 = jnp.sum(jnp.where(oh1, pos, 0.0), axis=-1, keepdims=True)
    rank2 = jnp.sum(jnp.where(oh2, pos, 0.0), axis=-1, keepdims=True)
    cnt_ref[...] = cnt_ref[...] + jnp.sum(oh, axis=0, keepdims=True)

    mi = jnp.where(lane == 0, eid1, jnp.where(lane == 1, eid2, jnp.where(lane == 2, rank1,
                                                                         jnp.where(lane == 3, rank2, 0.0))))
    mi_ref[...] = mi.astype(jnp.int32)
    mf_ref[...] = jnp.where(lane == 0, gate1, jnp.where(lane == 1, gate2, 0.0))


def _router(x, g, w_all, b_all, base, *, tm):
    m = x.shape[0]
    row = pl.BlockSpec((tm, LANES), lambda i: (i, 0))
    one = pl.BlockSpec((1, LANES), lambda i: (0, 0))
    return pl.pallas_call(
        _router_kernel,
        grid=(m // tm,),
        in_specs=[
            pl.BlockSpec((tm, D_MODEL), lambda i: (i, 0)),
            pl.BlockSpec((1, D_MODEL), lambda i: (0, 0)),
            pl.BlockSpec((D_MODEL, LANES), lambda i: (0, 0)),
            one, one,
        ],
        out_specs=[pl.BlockSpec((tm, D_MODEL // 2), lambda i: (i, 0)), row, row, one],
        out_shape=[
            jax.ShapeDtypeStruct((m, D_MODEL // 2), jnp.int32),
            jax.ShapeDtypeStruct((m, LANES), jnp.int32),
            jax.ShapeDtypeStruct((m, LANES), F32),
            jax.ShapeDtypeStruct((1, LANES), F32),
        ],
        compiler_params=_cparams(("arbitrary",)),
        name="moe_router",
    )(x, g.reshape(1, D_MODEL), w_all, b_all, base)


def _gather_rows(idx_ref, base, n_rows, src_hbm, dst, sem):
    def body(r, carry):
        t = idx_ref[base + r]
        pltpu.make_async_copy(src_hbm.at[pl.ds(t, 1)], dst.at[pl.ds(r, 1)], sem).start()
        return carry
    lax.fori_loop(0, n_rows, body, 0, unroll=8)


def _wait_rows(n_rows, src_hbm, dst, sem):
    pltpu.make_async_copy(src_hbm.at[pl.ds(0, n_rows)], dst, sem).wait()


def _moe_ffn_kernel(be_ref, nu_ref, tok_ref, h_hbm, wg_ref, wu_ref, wd_ref, y_ref, xbuf, sem):
    i = pl.program_id(0)
    n_used = nu_ref[0]

    def fetch(block):
        slot = block % 2
        _gather_rows(tok_ref, block * MOE_ROWS, MOE_ROWS, h_hbm, xbuf.at[slot], sem.at[slot])

    @pl.when(i == 0)
    def _():
        fetch(0)

    @pl.when(i + 1 < n_used)
    def _():
        fetch(i + 1)

    @pl.when(i < n_used)
    def _():
        slot = i % 2
        _wait_rows(MOE_ROWS, h_hbm, xbuf.at[slot], sem.at[slot])
        x = _unpack_bf16_pairs(xbuf[slot])
        gate = _dot(x, wg_ref[0, 0].astype(BF16))
        up = _dot(x, wu_ref[0, 0].astype(BF16))
        act = (gate * jax.nn.sigmoid(gate) * up).astype(BF16)
        y_ref[...] = _dot(act, wd_ref[0, 0].astype(BF16))

    @pl.when(i >= n_used)
    def _():
        y_ref[...] = jnp.zeros_like(y_ref)


def _moe_ffn(h_all, tok_of_slot, blk_expert, n_used, w_gate, w_up, w_down, layer):
    n_blocks = tok_of_slot.shape[0] // MOE_ROWS
    rows = lambda i, be, nu, tok: (i, 0)
    wmap = lambda i, be, nu, tok: (layer, be[i], 0, 0)
    return pl.pallas_call(
        _moe_ffn_kernel,
        grid_spec=pltpu.PrefetchScalarGridSpec(
            num_scalar_prefetch=3,
            grid=(n_blocks,),
            in_specs=[
                pl.BlockSpec(memory_space=pl.ANY),
                pl.BlockSpec((1, 1, D_MODEL, E_FF), wmap),
                pl.BlockSpec((1, 1, D_MODEL, E_FF), wmap),
                pl.BlockSpec((1, 1, E_FF, D_MODEL), wmap),
            ],
            out_specs=pl.BlockSpec((MOE_ROWS, D_MODEL), rows),
            scratch_shapes=[pltpu.VMEM((2, MOE_ROWS, D_MODEL // 2), jnp.int32), pltpu.SemaphoreType.DMA((2,))],
        ),
        out_shape=jax.ShapeDtypeStruct((n_blocks * MOE_ROWS, D_MODEL), F32),
        compiler_params=_cparams(("arbitrary",)),
        name="moe_ffn",
    )(blk_expert, n_used, tok_of_slot, h_all, w_gate, w_up, w_down)


def _moe_combine_kernel(dest_ref, x_ref, g_ref, y_hbm, o_ref, ybuf, sem, *, tm, n_tiles):
    i = pl.program_id(0)

    def fetch(tile):
        slot = tile % 2
        for k in range(2):
            _gather_rows(dest_ref, (k * n_tiles + tile) * tm, tm, y_hbm, ybuf.at[slot, k], sem.at[slot, k])

    @pl.when(i == 0)
    def _():
        fetch(0)

    @pl.when(i + 1 < n_tiles)
    def _():
        fetch(i + 1)

    slot = i % 2
    acc = x_ref[...]
    for k in range(2):
        _wait_rows(tm, y_hbm, ybuf.at[slot, k], sem.at[slot, k])
        acc = acc + g_ref[:, k:k + 1] * ybuf[slot, k]
    o_ref[...] = acc


def _moe_combine(x, gates, dest, y_rows, *, tm):
    m = x.shape[0]
    n_tiles = m // tm
    return pl.pallas_call(
        functools.partial(_moe_combine_kernel, tm=tm, n_tiles=n_tiles),
        grid_spec=pltpu.PrefetchScalarGridSpec(
            num_scalar_prefetch=1,
            grid=(n_tiles,),
            in_specs=[
                pl.BlockSpec((tm, D_MODEL), lambda i, d: (i, 0)),
                pl.BlockSpec((tm, LANES), lambda i, d: (i, 0)),
                pl.BlockSpec(memory_space=pl.ANY),
            ],
            out_specs=pl.BlockSpec((tm, D_MODEL), lambda i, d: (i, 0)),
            scratch_shapes=[pltpu.VMEM((2, 2, tm, D_MODEL), F32), pltpu.SemaphoreType.DMA((2, 2))],
        ),
        out_shape=jax.ShapeDtypeStruct((m, D_MODEL), F32),
        compiler_params=_cparams(("arbitrary",)),
        name="moe_combine",
    )(dest.reshape(-1), x, gates, y_rows)


def _moe(xp, xs, g, w_rg, b_rg, w_re, b_re, w_gate, w_up, w_down, layer):
    n_p, n_s = xp.shape[0], xs.shape[0]
    n_tok = n_p + n_s
    pad = LANES - E_GROUPS - N_EXPERTS
    w_all = jnp.pad(jnp.concatenate([w_rg, w_re.reshape(D_MODEL, N_EXPERTS)], axis=1), ((0, 0), (0, pad)))
    b_all = jnp.pad(jnp.concatenate([b_rg, b_re.reshape(N_EXPERTS)]), (0, pad)).reshape(1, LANES)
    hp, mip, mfp, cnt_p = _router(xp, g, w_all, b_all, jnp.zeros((1, LANES), F32), tm=512)
    hs, mis, mfs, cnt = _router(xs, g, w_all, b_all, cnt_p, tm=n_s)
    h_all = jnp.concatenate([hp, hs], axis=0)
    mi = jnp.concatenate([mip, mis], axis=0)
    eid, rank = mi[:, 0:2], mi[:, 2:4]

    counts = cnt[0, :N_EXPERTS].astype(jnp.int32)
    padded = (counts + MOE_ROWS - 1) // MOE_ROWS * MOE_ROWS
    pad_end = jnp.cumsum(padded)
    pad_start = pad_end - padded
    n_blocks = (n_tok * 2) // MOE_ROWS + N_EXPERTS
    n_used = pad_end[-1] // MOE_ROWS
    blk_id = jnp.minimum(jnp.arange(n_blocks), n_used - 1)
    blk_expert = jnp.minimum(jnp.searchsorted(pad_end, blk_id * MOE_ROWS, side='right'), N_EXPERTS - 1)
    dest = pad_start[eid] + rank
    tok_of_slot = jnp.zeros((n_blocks * MOE_ROWS,), jnp.int32).at[dest.reshape(-1)].set(
        jnp.repeat(jnp.arange(n_tok, dtype=jnp.int32), 2))
    y_rows = _moe_ffn(h_all, tok_of_slot, blk_expert.astype(jnp.int32), n_used.reshape(1).astype(jnp.int32),
                      w_gate, w_up, w_down, layer)
    dest_t = jnp.transpose(dest)
    return (_moe_combine(xp, mfp, dest_t[:, :n_p], y_rows, tm=MOE_ROWS),
            _moe_combine(xs, mfs, dest_t[:, n_p:], y_rows, tm=n_s))


def _final_norm_kernel(x_ref, g_ref, o_ref):
    o_ref[...] = _rmsnorm_rows(x_ref[...], g_ref[...])


def _final_norm(x, g, *, tm):
    m = x.shape[0]
    return pl.pallas_call(
        _final_norm_kernel,
        grid=(m // tm,),
        in_specs=[pl.BlockSpec((tm, D_MODEL), lambda i: (i, 0)), pl.BlockSpec((1, D_MODEL), lambda i: (0, 0))],
        out_specs=pl.BlockSpec((tm, D_MODEL), lambda i: (i, 0)),
        out_shape=jax.ShapeDtypeStruct((m, D_MODEL), F32),
        compiler_params=_cparams(("parallel",)),
        name="final_norm",
    )(x, g.reshape(1, D_MODEL))


def _rope_tables(pos):
    half = HEAD_DIM // 2
    inv = 1.0 / (ROPE_THETA ** (jnp.arange(half, dtype=F32) / half))
    ang = pos.astype(F32)[:, None] * inv[None, :]
    cos, sin = jnp.cos(ang), jnp.sin(ang)
    return jnp.concatenate([cos, cos], -1), jnp.concatenate([-sin, sin], -1)


def _kv_state(proj3, group, keep):
    b, t, _ = proj3.shape
    base = 2 * A_WIDTH + group * 3 * B_HW
    k = proj3[:, t - keep:, base + B_HW: base + 2 * B_HW]
    v = proj3[:, t - keep:, base + 2 * B_HW: base + 3 * B_HW]
    return jnp.stack([k, v], axis=2).reshape(b, keep, 2, B_HEADS, HEAD_DIM)


def _layer_ab(xp, xs, g, w_in, ln_g, ln_b, w_s, b_s, w_out, caches, tabs_p, tabs_s, shapes):
    bp, tp, bs, ts = shapes
    w_in_b = w_in.astype(BF16)
    w_out_a = w_out[:A_WIDTH].astype(BF16)
    w_out_b = w_out[A_WIDTH:].astype(BF16)

    proj_p = _inproj(xp, g, w_in_b, tabs_p, tm=1024)
    a_p = _gmlp(proj_p, ln_g, ln_b, w_s, b_s, n_rows=bp * tp, chunks=2, emit_v=False)[0]
    outs, lses = zip(*[_attn_prompt(proj_p, grp, batch=bp, seq=tp) for grp in range(len(B_CONFIGS))])
    b_p = _merge(outs, lses, tm=1024)
    xp_new = _outproj([a_p, b_p], [w_out_a, w_out_b], xp, tm=1024, tn=1024)
    proj_p3 = proj_p.reshape(bp, tp, AB_IN)
    kv_p = [_kv_state(proj_p3, grp, min(w, tp)) for grp, (w, _) in enumerate(B_CONFIGS)]

    n_s = bs * ts
    proj_s = _inproj(xs, g, w_in_b, tabs_s, tm=n_s)
    proj_s3 = proj_s.reshape(bs, ts, AB_IN)
    chunk_in = jnp.pad(proj_s3[:, :, :2 * A_WIDTH], ((0, 0), (0, A_CHUNK - ts), (0, 0)))
    a_s, v_s = _gmlp(chunk_in.reshape(bs * A_CHUNK, 2 * A_WIDTH), ln_g, ln_b, w_s, b_s,
                     n_rows=bs * A_CHUNK, chunks=1, emit_v=True)
    a_s = a_s.reshape(bs, A_CHUNK, A_WIDTH)[:, :ts].reshape(n_s, A_WIDTH)
    v_s = v_s.reshape(bs, A_CHUNK, A_WIDTH)[:, :ts]
    qkv_s = jnp.pad(proj_s3[:, :, 2 * A_WIDTH:], ((0, 0), (0, SAMPLE_ROWS - ts), (0, 0)))
    b_s_out = _attn_sample(qkv_s, caches, n_real=ts)[:, :ts].reshape(n_s, B_HW)
    xs_new = _outproj([a_s, b_s_out], [w_out_a, w_out_b], xs, tm=n_s, tn=1024)
    kv_s = []
    for grp, c in enumerate(caches):
        new = _kv_state(proj_s3, grp, ts)
        kv_s.append(jnp.concatenate([c, new], axis=1)[:, ts:])
    return xp_new, xs_new, kv_p, kv_s, v_s


def _layer_c(xp, xs, g, w_in, conv_w, conv_b, dt_bias, a_log, d_skip, norm_g, w_out,
             conv_state, ssm_state, shapes):
    bp, tp, bs, ts = shapes
    n_main = C_D_INNER + C_CONV_DIM
    w_main = w_in[:, :n_main].astype(BF16)
    w_dt = jnp.pad(w_in[:, n_main:], ((0, 0), (0, LANES - C_HEADS))).astype(BF16)
    w_out_b = w_out.astype(BF16)
    args = (conv_w, conv_b, dt_bias, a_log, d_skip, norm_g)

    proj_p = _inproj(xp, g, w_main, tm=1024)
    dt_p = _inproj(xp, g, w_dt, tm=1024)
    zc = jnp.zeros((bp, CONV_PAD, C_CONV_DIM), F32)
    zh = jnp.zeros((bp, C_GROUPS, C_D_STATE, C_GW), F32)
    q_p = 128
    y_p, h_p = _ssd(proj_p, dt_p, zc, zh, *args, batch=bp, n_chunks=tp // q_p, q_len=q_p, t_valid=q_p)
    xp_new = _outproj([y_p], [w_out_b], xp, tm=512, tn=1024)
    conv_p = proj_p.reshape(bp, tp, n_main)[:, tp - (C_CONV - 1):, C_D_INNER:]

    n_s = bs * ts
    q_s = 128
    proj_s = _inproj(xs, g, w_main, tm=n_s)
    dt_s = _inproj(xs, g, w_dt, tm=n_s)
    pad_rows = lambda a: jnp.pad(a.reshape(bs, ts, -1), ((0, 0), (0, q_s - ts), (0, 0))).reshape(bs * q_s, -1)
    cst = jnp.pad(conv_state, ((0, 0), (CONV_PAD - (C_CONV - 1), 0), (0, 0)))
    y_s, h_s = _ssd(pad_rows(proj_s), pad_rows(dt_s), cst, _state_to_t(ssm_state), *args,
                    batch=bs, n_chunks=1, q_len=q_s, t_valid=ts)
    y_s = y_s.reshape(bs, q_s, C_D_INNER)[:, :ts].reshape(n_s, C_D_INNER)
    xs_new = _outproj([y_s], [w_out_b], xs, tm=n_s, tn=1024)
    assert ts >= C_CONV - 1
    conv_s = proj_s.reshape(bs, ts, n_main)[:, ts - (C_CONV - 1):, C_D_INNER:]
    return xp_new, xs_new, conv_p, conv_s, _state_from_t(h_p), _state_from_t(h_s)


def kernel(x_prompt, x_sample, cache_kv_w128, cache_kv_w512, cache_kv_w2048, state_conv, state_ssm,
           norm_mix, norm_ffn, norm_final, w_in_ab, a_ln_g, a_ln_b, a_w_s, a_b_s, w_out_ab,
           w_in_c, c_conv_w, c_conv_b, c_dt_bias, c_a_log, c_d, c_norm_g, w_out_c,
           w_router_g, b_router_g, w_router_e, b_router_e, w_exp_gate, w_exp_up, w_exp_down):
    bp, tp, _ = x_prompt.shape
    bs, ts, _ = x_sample.shape
    shapes = (bp, tp, bs, ts)
    depth = norm_mix.shape[0]
    kv_caches = (cache_kv_w128, cache_kv_w512, cache_kv_w2048)
    tabs_p = _rope_tables(jnp.arange(tp, dtype=jnp.int32))
    pos_s = PAST_LEN + jnp.arange(ts, dtype=jnp.int32)
    tabs_s = tuple(jnp.tile(t, (bs, 1)) for t in _rope_tables(pos_s))

    xp = x_prompt.reshape(bp * tp, D_MODEL)
    xs = x_sample.reshape(bs * ts, D_MODEL)
    kv_p = [[] for _ in B_CONFIGS]
    kv_s = [[] for _ in B_CONFIGS]
    chunk_v, conv_p, conv_s, ssm_p, ssm_s = [], [], [], [], []
    for l in range(depth):
        i = l // 2
        if l % 2 == 0:
            xp, xs, nkv_p, nkv_s, v_s = _layer_ab(
                xp, xs, norm_mix[l], w_in_ab[i], a_ln_g[i], a_ln_b[i], a_w_s[i], a_b_s[i], w_out_ab[i],
                tuple(c[i] for c in kv_caches), tabs_p, tabs_s, shapes)
            for grp in range(len(B_CONFIGS)):
                kv_p[grp].append(nkv_p[grp])
                kv_s[grp].append(nkv_s[grp])
            chunk_v.append(v_s)
        else:
            xp, xs, ncp, ncs, nsp, nss = _layer_c(
                xp, xs, norm_mix[l], w_in_c[i], c_conv_w[i], c_conv_b[i], c_dt_bias[i], c_a_log[i], c_d[i],
                c_norm_g[i], w_out_c[i], state_conv[i], state_ssm[i], shapes)
            conv_p.append(ncp)
            conv_s.append(ncs)
            ssm_p.append(nsp)
            ssm_s.append(nss)
        xp, xs = _moe(xp, xs, norm_ffn[l], w_router_g[l], b_router_g[l], w_router_e[l], b_router_e[l],
                      w_exp_gate, w_exp_up, w_exp_down, l)
    y_p = _final_norm(xp, norm_final, tm=1024).reshape(bp, tp, D_MODEL)
    y_s = _final_norm(xs, norm_final, tm=bs * ts).reshape(bs, ts, D_MODEL)
    return (y_p, y_s,
            jnp.stack(kv_p[0]), jnp.stack(kv_s[0]),
            jnp.stack(kv_p[1]), jnp.stack(kv_s[1]),
            jnp.stack(kv_p[2]), jnp.stack(kv_s[2]),
            jnp.stack(chunk_v),
            jnp.stack(conv_p), jnp.stack(conv_s),
            jnp.stack(ssm_p), jnp.stack(ssm_s))
```

```python
import functools
import math

import jax
import jax.numpy as jnp
from jax import lax
from jax.experimental import pallas as pl
from jax.experimental.pallas import tpu as pltpu

F32 = jnp.float32
BF16 = jnp.bfloat16

D_MODEL = 2048
PAST_LEN = 16384
EPS = 1e-6
A_WIDTH = 1024
A_GROUPS = 8
A_GW = 128
A_CHUNK = 128
HEAD_DIM = 128
B_HEADS = 4
B_CONFIGS = ((128, 1), (512, 4), (2048, 16))
B_BAND = 128
B_HW = B_HEADS * HEAD_DIM
AB_IN = 2 * A_WIDTH + 3 * len(B_CONFIGS) * B_HW
ROPE_THETA = 10000.0
C_D_INNER = 4096
C_HEAD_DIM = 64
C_HEADS = 64
C_GROUPS = 8
C_D_STATE = 128
C_CONV = 4
C_CONV_DIM = C_D_INNER + 2 * C_GROUPS * C_D_STATE
C_GW = C_D_INNER // C_GROUPS
E_GROUPS = 8
E_PER_GROUP = 8
N_EXPERTS = 64
E_FF = 512
MOE_ROWS = 256
GATHER_ROWS = 32
COMBINE_ROWS = 128
META_ROWS = 8

LANES = 128
VMEM_LIMIT = 56 * 1024 * 1024


def _cparams(sem, unchecked_dma=False):
    return pltpu.CompilerParams(dimension_semantics=sem, vmem_limit_bytes=VMEM_LIMIT,
                                disable_bounds_checks=unchecked_dma)


def _split3(x):
    hi = x.astype(BF16)
    r1 = x - hi.astype(F32)
    mid = r1.astype(BF16)
    lo = (r1 - mid.astype(F32)).astype(BF16)
    return hi, mid, lo


def _dot(a, b):
    return jnp.dot(a, b, preferred_element_type=F32)


def _dot_nt(a, b):
    return lax.dot_general(a, b, (((1,), (1,)), ((), ())), preferred_element_type=F32)


def _rmsnorm_rows(x, g):
    ms = jnp.mean(x * x, axis=-1, keepdims=True)
    return x * lax.rsqrt(ms + EPS) * g


def _inproj_kernel(x_ref, g_ref, w_ref, *rest, rope):
    o_ref, h_scr = rest[-2:]
    j = pl.program_id(1)

    @pl.when(j == 0)
    def _():
        h_scr[...] = _rmsnorm_rows(x_ref[...], g_ref[...]).astype(BF16)

    acc = _dot(h_scr[...], w_ref[0])
    if not rope:
        o_ref[...] = acc
        return
    cos_ref, sin_ref = rest[:2]
    is_qk = jnp.logical_and(j >= 4, (j - 4) % 3 != 2)

    @pl.when(is_qk)
    def _():
        cos = cos_ref[...]
        sin = sin_ref[...]
        for h in range(B_HEADS):
            sl = slice(h * HEAD_DIM, (h + 1) * HEAD_DIM)
            a = acc[:, sl]
            o_ref[:, sl] = a * cos + pltpu.roll(a, HEAD_DIM // 2, 1) * sin

    @pl.when(jnp.logical_not(is_qk))
    def _():
        o_ref[...] = acc


def _inproj(x, g, w_bf16, layer, tabs=None, *, tm, n_out=None):
    m, _ = x.shape
    n = n_out or w_bf16.shape[2]
    tn = B_HW if n % B_HW == 0 else n
    in_specs = [
        pl.BlockSpec((tm, D_MODEL), lambda i, j: (i, 0)),
        pl.BlockSpec((1, D_MODEL), lambda i, j: (0, 0)),
        pl.BlockSpec((1, D_MODEL, tn), lambda i, j: (layer, 0, j)),
    ]
    args = [x, g.reshape(1, D_MODEL), w_bf16]
    if tabs is not None:
        n_pos_tiles = tabs[0].shape[0] // tm
        in_specs += [pl.BlockSpec((tm, HEAD_DIM), lambda i, j: (i % n_pos_tiles, 0))] * 2
        args += list(tabs)
    return pl.pallas_call(
        functools.partial(_inproj_kernel, rope=tabs is not None),
        grid=(m // tm, n // tn),
        in_specs=in_specs,
        out_specs=pl.BlockSpec((tm, tn), lambda i, j: (i, j)),
        out_shape=jax.ShapeDtypeStruct((m, n), F32),
        scratch_shapes=[pltpu.VMEM((tm, D_MODEL), BF16)],
        compiler_params=_cparams(("parallel", "arbitrary")),
        name="inproj",
    )(*args)


def _gmlp_kernel(p_ref, lng_ref, lnb_ref, ws_ref, bs_ref, a_ref, *v_ref, chunks):
    row = lax.broadcasted_iota(jnp.int32, (A_CHUNK, A_CHUNK), 0)
    col = lax.broadcasted_iota(jnp.int32, (A_CHUNK, A_CHUNK), 1)
    causal = col <= row
    for c in range(chunks):
        rows = slice(c * A_CHUNK, (c + 1) * A_CHUNK)
        u = jax.nn.gelu(p_ref[rows, 0:A_WIDTH], approximate=True)
        vg = jax.nn.gelu(p_ref[rows, A_WIDTH:2 * A_WIDTH], approximate=True)
        vc = vg - jnp.mean(vg, axis=-1, keepdims=True)
        v = vc * lax.rsqrt(jnp.mean(vc * vc, axis=-1, keepdims=True) + EPS)
        v = v * lng_ref[...] + lnb_ref[...]
        if v_ref:
            v_ref[0][rows, :] = v
        vb = v.astype(BF16)
        for g in range(A_GROUPS):
            sl = slice(g * A_GW, (g + 1) * A_GW)
            w = jnp.where(causal, ws_ref[g], 0.0).astype(BF16)
            mixed = _dot(w, vb[:, sl]) + bs_ref[:, sl]
            a_ref[rows, sl] = (u[:, sl] * mixed).astype(BF16)


def _gmlp(proj, ln_g, ln_b, w_s, b_s, *, n_rows, chunks, emit_v):
    rows = chunks * A_CHUNK
    bsb = jnp.repeat(jnp.transpose(b_s), A_GW, axis=1)
    out_shape = [jax.ShapeDtypeStruct((n_rows, A_WIDTH), BF16)]
    out_specs = [pl.BlockSpec((rows, A_WIDTH), lambda i: (i, 0))]
    if emit_v:
        out_shape.append(jax.ShapeDtypeStruct((n_rows, A_WIDTH), F32))
        out_specs.append(pl.BlockSpec((rows, A_WIDTH), lambda i: (i, 0)))
    return pl.pallas_call(
        functools.partial(_gmlp_kernel, chunks=chunks),
        grid=(n_rows // rows,),
        in_specs=[
            pl.BlockSpec((rows, 2 * A_WIDTH), lambda i: (i, 0)),
            pl.BlockSpec((1, A_WIDTH), lambda i: (0, 0)),
            pl.BlockSpec((1, A_WIDTH), lambda i: (0, 0)),
            pl.BlockSpec((A_GROUPS, A_CHUNK, A_CHUNK), lambda i: (0, 0, 0)),
            pl.BlockSpec((A_CHUNK, A_WIDTH), lambda i: (0, 0)),
        ],
        out_specs=out_specs,
        out_shape=out_shape,
        compiler_params=_cparams(("parallel",)),
        name="gmlp",
    )(proj, ln_g.reshape(1, A_WIDTH), ln_b.reshape(1, A_WIDTH), w_s, bsb)


def _attn_prompt_kernel(q_ref, kp_ref, kc_ref, vp_ref, vc_ref, o_ref, lse_ref, *, dil, heads):
    n = pl.program_id(2)
    qi = lax.broadcasted_iota(jnp.int32, (B_BAND, B_BAND), 0)
    kj = lax.broadcasted_iota(jnp.int32, (B_BAND, B_BAND), 1)
    prev_ok = jnp.logical_and(kj >= qi, n > 0)
    cur_ok = kj <= qi
    scale = HEAD_DIM ** -0.5
    for r in range(dil):
        rows = pl.ds(r, B_BAND, stride=dil) if dil > 1 else slice(None)
        for h in range(heads):
            sl = slice(h * HEAD_DIM, (h + 1) * HEAD_DIM)
            q = q_ref[rows, sl].astype(BF16)
            sp = _dot_nt(q, kp_ref[rows, sl].astype(BF16)) * scale
            sc = _dot_nt(q, kc_ref[rows, sl].astype(BF16)) * scale
            sp = jnp.where(prev_ok, sp, -jnp.inf)
            sc = jnp.where(cur_ok, sc, -jnp.inf)
            m = jnp.maximum(jnp.max(sp, axis=-1, keepdims=True), jnp.max(sc, axis=-1, keepdims=True))
            pp = jnp.exp(sp - m)
            pc = jnp.exp(sc - m)
            l = jnp.sum(pp, axis=-1, keepdims=True) + jnp.sum(pc, axis=-1, keepdims=True)
            o = (_dot(pp.astype(BF16), vp_ref[rows, sl].astype(BF16))
                 + _dot(pc.astype(BF16), vc_ref[rows, sl].astype(BF16)))
            o_ref[rows, sl] = o / l
            lse_ref[rows, sl] = jnp.broadcast_to(m + jnp.log(l), (B_BAND, HEAD_DIM))


def _attn_prompt(proj, group, *, batch, seq):
    dil = B_CONFIGS[group][1]
    rows = B_BAND * dil
    nblk = seq // rows
    heads = B_HEADS if dil == 1 else 1
    width = heads * HEAD_DIM
    n_col = B_HW // width
    qcol = (4 + 3 * group) * n_col

    def spec(off, prev):
        def imap(b, h, n):
            nn = jnp.maximum(n - 1, 0) if prev else n
            return (b * nblk + nn, qcol + off * n_col + h)
        return pl.BlockSpec((rows, width), imap)

    out_spec = pl.BlockSpec((rows, width), lambda b, h, n: (b * nblk + n, h))
    return pl.pallas_call(
        functools.partial(_attn_prompt_kernel, dil=dil, heads=heads),
        grid=(batch, n_col, nblk),
        in_specs=[spec(0, False), spec(1, True), spec(1, False), spec(2, True), spec(2, False)],
        out_specs=[out_spec, out_spec],
        out_shape=[jax.ShapeDtypeStruct((batch * seq, B_HW), F32)] * 2,
        compiler_params=_cparams(("parallel", "parallel", "arbitrary")),
        name=f"attn_prompt_g{group}",
    )(proj, proj, proj, proj, proj)


def _merge_kernel(o0, o1, o2, l0, l1, l2, b_ref):
    m = jnp.maximum(jnp.maximum(l0[...], l1[...]), l2[...])
    e0 = jnp.exp(l0[...] - m)
    e1 = jnp.exp(l1[...] - m)
    e2 = jnp.exp(l2[...] - m)
    den = e0 + e1 + e2
    b_ref[...] = ((e0 / den) * o0[...] + (e1 / den) * o1[...] + (e2 / den) * o2[...]).astype(BF16)


def _merge(outs, lses, *, tm):
    m = outs[0].shape[0]
    spec = pl.BlockSpec((tm, B_HW), lambda i: (i, 0))
    return pl.pallas_call(
        _merge_kernel,
        grid=(m // tm,),
        in_specs=[spec] * 6,
        out_specs=spec,
        out_shape=jax.ShapeDtypeStruct((m, B_HW), BF16),
        compiler_params=_cparams(("parallel",)),
        name="attn_merge",
    )(*outs, *lses)


SAMPLE_ROWS = 16


def _attn_sample_kernel(qkv_ref, c0_ref, c1_ref, c2_ref, b_ref, *, n_real):
    caches = (c0_ref, c1_ref, c2_ref)
    row_w = 2 * B_HW
    scale = HEAD_DIM ** -0.5
    qi_c = lax.broadcasted_iota(jnp.int32, (SAMPLE_ROWS, B_BAND), 0)
    ka_c = lax.broadcasted_iota(jnp.int32, (SAMPLE_ROWS, B_BAND), 1)
    qi_n = lax.broadcasted_iota(jnp.int32, (SAMPLE_ROWS, SAMPLE_ROWS), 0)
    km_n = lax.broadcasted_iota(jnp.int32, (SAMPLE_ROWS, SAMPLE_ROWS), 1)

    def valid(diff, dil):
        shift = dil.bit_length() - 1
        ok = jnp.logical_and(diff >= 0, (diff & (dil - 1)) == 0)
        return jnp.logical_and(ok, (diff >> shift) <= B_BAND)

    for h in range(B_HEADS):
        hs = slice(h * HEAD_DIM, (h + 1) * HEAD_DIM)
        outs, lses = [], []
        for g, (window, dil) in enumerate(B_CONFIGS):
            base = g * 3 * B_HW
            q = qkv_ref[0, :, base + h * HEAD_DIM: base + (h + 1) * HEAD_DIM].astype(BF16)
            kn = qkv_ref[0, :, base + B_HW + h * HEAD_DIM: base + B_HW + (h + 1) * HEAD_DIM].astype(BF16)
            vn = qkv_ref[0, :, base + 2 * B_HW + h * HEAD_DIM: base + 2 * B_HW + (h + 1) * HEAD_DIM].astype(BF16)
            n_res = min(dil, n_real)
            scores, values = [], []
            for r in range(n_res):
                kc = caches[g][0, 0, :, r * row_w + h * HEAD_DIM: r * row_w + (h + 1) * HEAD_DIM].astype(BF16)
                vc = caches[g][0, 0, :, r * row_w + B_HW + h * HEAD_DIM: r * row_w + B_HW + (h + 1) * HEAD_DIM].astype(BF16)
                s = _dot_nt(q, kc) * scale
                diff = qi_c + window - ka_c * dil - r
                scores.append(jnp.where(valid(diff, dil), s, -jnp.inf))
                values.append(vc)
            s = _dot_nt(q, kn) * scale
            scores.append(jnp.where(valid(qi_n - km_n, dil), s, -jnp.inf))
            values.append(vn)
            m = functools.reduce(jnp.maximum, [jnp.max(s, axis=-1, keepdims=True) for s in scores])
            ps = [jnp.exp(s - m) for s in scores]
            l = functools.reduce(jnp.add, [jnp.sum(p, axis=-1, keepdims=True) for p in ps])
            o = functools.reduce(jnp.add, [_dot(p.astype(BF16), v) for p, v in zip(ps, values)])
            outs.append(o / l)
            lses.append(m + jnp.log(l))
        m = functools.reduce(jnp.maximum, lses)
        es = [jnp.exp(l - m) for l in lses]
        den = functools.reduce(jnp.add, es)
        b_ref[0, :, hs] = functools.reduce(jnp.add, [(e / den) * o for e, o in zip(es, outs)]).astype(BF16)


def _cache_views(caches, n_real):
    views = []
    for (window, dil), c in zip(B_CONFIGS, caches):
        layers, b = c.shape[:2]
        assert c.shape[2] == window and window == dil * B_BAND
        n_res = min(dil, n_real)
        c = c.reshape(layers, b, B_BAND, dil, 2, B_HEADS, HEAD_DIM)[:, :, :, :n_res]
        views.append(c.reshape(layers, b, B_BAND, n_res * 2 * B_HW))
    return views


def _attn_sample(qkv, views, layer):
    b = qkv.shape[0]
    n_real = max(v.shape[3] // (2 * B_HW) for v in views)
    specs = [pl.BlockSpec((1, 1) + v.shape[2:], lambda i: (layer, i, 0, 0)) for v in views]
    return pl.pallas_call(
        functools.partial(_attn_sample_kernel, n_real=n_real),
        grid=(b,),
        in_specs=[pl.BlockSpec((1, SAMPLE_ROWS, qkv.shape[2]), lambda i: (i, 0, 0))] + specs,
        out_specs=pl.BlockSpec((1, SAMPLE_ROWS, B_HW), lambda i: (i, 0, 0)),
        out_shape=jax.ShapeDtypeStruct((b, SAMPLE_ROWS, B_HW), BF16),
        compiler_params=_cparams(("parallel",)),
        name="attn_sample",
    )(qkv, *views)


def _outproj_kernel(*refs, n_parts):
    a_refs = refs[:n_parts]
    w_refs = refs[n_parts:2 * n_parts]
    x_ref, o_ref = refs[2 * n_parts], refs[2 * n_parts + 1]
    acc = x_ref[...]
    for a, w in zip(a_refs, w_refs):
        acc = acc + _dot(a[...], w[0])
    o_ref[...] = acc


def _outproj(parts, w_bf16, layer, x, *, tm, tn):
    m, n = x.shape
    in_specs = [pl.BlockSpec((tm, a.shape[1]), lambda j, i: (i, 0)) for a in parts]
    row0 = 0
    for a in parts:
        width = a.shape[1]
        blk = row0 // width
        assert blk * width == row0
        in_specs.append(pl.BlockSpec((1, width, tn), lambda j, i, blk=blk: (layer, blk, j)))
        row0 += width
    assert row0 == w_bf16.shape[1]
    weights = [w_bf16] * len(parts)
    in_specs.append(pl.BlockSpec((tm, tn), lambda j, i: (i, j)))
    return pl.pallas_call(
        functools.partial(_outproj_kernel, n_parts=len(parts)),
        grid=(n // tn, m // tm),
        in_specs=in_specs,
        out_specs=pl.BlockSpec((tm, tn), lambda j, i: (i, j)),
        out_shape=jax.ShapeDtypeStruct((m, n), F32),
        compiler_params=_cparams(("parallel", "parallel")),
        name="outproj",
    )(*parts, *weights, x)


CONV_PAD = 8


def _ssd_kernel(z_ref, xa_ref, xb_ref, bc_ref, dtr_ref, dtrt_ref, cst_ref, cw_ref, cb_ref,
                dtb_ref, dtbc_ref, alog_ref, alogc_ref, dsk_ref, ng_ref, e_ref, h0_ref,
                y_ref, h_ref, ext_scr, act_scr, *, q_len, t_valid):
    c = pl.program_id(1)

    @pl.when(c == 0)
    def _():
        ext_scr[0:CONV_PAD, :] = cst_ref[0]
        h_ref[...] = h0_ref[...]

    half = C_CONV_DIM // 3
    ext_scr[CONV_PAD:CONV_PAD + q_len, 0:half] = xa_ref[...]
    ext_scr[CONV_PAD:CONV_PAD + q_len, half:2 * half] = xb_ref[...]
    ext_scr[CONV_PAD:CONV_PAD + q_len, 2 * half:3 * half] = bc_ref[...]
    for s in range(C_CONV_DIM // C_GW):
        sl = slice(s * C_GW, (s + 1) * C_GW)
        acc = cb_ref[:, sl]
        for j in range(C_CONV):
            lo = CONV_PAD - (C_CONV - 1) + j
            acc = acc + ext_scr[lo:lo + q_len, sl] * cw_ref[j:j + 1, sl]
        act_scr[:, sl] = acc * jax.nn.sigmoid(acc)
    ext_scr[0:CONV_PAD, :] = ext_scr[q_len:q_len + CONV_PAD, :]

    ti = lax.broadcasted_iota(jnp.int32, (q_len, q_len), 0)
    si = lax.broadcasted_iota(jnp.int32, (q_len, q_len), 1)
    tri = si <= ti
    tri_b = jnp.where(tri, 1.0, 0.0).astype(BF16)
    tri_tb = jnp.where(ti <= si, 1.0, 0.0).astype(BF16)

    dt = jax.nn.softplus(dtr_ref[...] + dtb_ref[...])
    dtt = jax.nn.softplus(dtrt_ref[...] + dtbc_ref[...])
    if t_valid < q_len:
        dt = jnp.where(lax.broadcasted_iota(jnp.int32, dt.shape, 0) < t_valid, dt, 0.0)
        dtt = jnp.where(lax.broadcasted_iota(jnp.int32, dtt.shape, 1) < t_valid, dtt, 0.0)
    da = dt * (-jnp.exp(alog_ref[...]))
    dat = dtt * (-jnp.exp(alogc_ref[...]))
    a_cum = functools.reduce(jnp.add, [_dot(tri_b, p) for p in reversed(_split3(da))])
    a_cumt = functools.reduce(jnp.add, [_dot(p, tri_tb) for p in reversed(_split3(dat))])
    dt_parts = _split3(dt)
    ac_parts = _split3(a_cum)

    lane_lo = lax.broadcasted_iota(jnp.int32, (q_len, LANES), 1) < C_HEAD_DIM

    for g in range(C_GROUPS):
        gs = slice(g * C_GW, (g + 1) * C_GW)
        e_g = e_ref[:, gs]
        dt_x = functools.reduce(jnp.add, [_dot(p, e_g) for p in reversed(dt_parts)])
        ac_x = functools.reduce(jnp.add, [_dot(p, e_g) for p in reversed(ac_parts)])
        xs = act_scr[:, gs]
        bm = act_scr[:, C_D_INNER + g * C_D_STATE: C_D_INNER + (g + 1) * C_D_STATE]
        cm = act_scr[:, C_D_INNER + C_GROUPS * C_D_STATE + g * C_D_STATE:
                     C_D_INNER + C_GROUPS * C_D_STATE + (g + 1) * C_D_STATE]
        bm_b = bm.astype(BF16)
        cm_b = cm.astype(BF16)
        xdt = xs * dt_x
        a_last = ac_x[q_len - 1:q_len, :]
        xdt_b = xdt.astype(BF16)
        xdt_end_b = (xdt * jnp.exp(a_last - ac_x)).astype(BF16)
        cbm = _dot_nt(cm_b, bm_b)

        h_prev = h_ref[0, g]
        y = _dot(cm_b, h_prev.astype(BF16)) * jnp.exp(ac_x)
        st = _dot(jnp.transpose(bm).astype(BF16), xdt_end_b)
        h_ref[0, g] = h_prev * jnp.exp(a_last) + st

        pairs = []
        for hp in range(C_GW // LANES):
            ms = []
            for hh in (2 * hp, 2 * hp + 1):
                hd = g * (C_GW // C_HEAD_DIM) + hh
                seg = a_cum[:, hd:hd + 1] - a_cumt[hd:hd + 1, :]
                decay = jnp.exp(jnp.where(tri, seg, -jnp.inf))
                ms.append((cbm * decay).astype(BF16))
            slab = xdt_b[:, hp * LANES:(hp + 1) * LANES]
            zero = jnp.zeros_like(slab)
            rhs = jnp.concatenate([jnp.where(lane_lo, slab, zero), jnp.where(lane_lo, zero, slab)], axis=0)
            pairs.append(_dot(jnp.concatenate(ms, axis=1), rhs))
        y = y + jnp.concatenate(pairs, axis=1)
        y = y + xs * dsk_ref[:, gs]
        zz = z_ref[:, gs]
        y = y * (zz * jax.nn.sigmoid(zz))
        y = y * lax.rsqrt(jnp.mean(y * y, axis=-1, keepdims=True) + EPS) * ng_ref[:, gs]
        y_ref[:, gs] = y.astype(BF16)


def _ssd(proj, dt_raw, conv_state, h0t, conv_w, conv_b, dt_bias, a_log, d_skip, norm_g,
         *, batch, n_chunks, q_len, t_valid):
    rows = batch * n_chunks * q_len
    dtrt = dt_raw.reshape(batch * n_chunks, q_len, LANES).transpose(0, 2, 1).reshape(-1, q_len)
    pad = LANES - C_HEADS
    col = lambda v: jnp.pad(v, (0, pad)).reshape(LANES, 1)
    row = lambda v: jnp.pad(v, (0, pad)).reshape(1, LANES)
    expand = jnp.repeat(jnp.eye(LANES, C_HEADS, dtype=BF16), C_HEAD_DIM, axis=1)
    cw = jnp.pad(conv_w, ((0, CONV_PAD - C_CONV), (0, 0)))
    xcol = C_D_INNER // (C_CONV_DIM // 3)
    const = lambda shape: pl.BlockSpec(shape, lambda b, c: (0,) * len(shape))
    blk = lambda width, j: pl.BlockSpec((q_len, width), lambda b, c: (b * n_chunks + c, j))
    y, h = pl.pallas_call(
        functools.partial(_ssd_kernel, q_len=q_len, t_valid=t_valid),
        grid=(batch, n_chunks),
        in_specs=[
            blk(C_D_INNER, 0), blk(C_CONV_DIM // 3, xcol), blk(C_CONV_DIM // 3, xcol + 1),
            blk(C_CONV_DIM // 3, xcol + 2), blk(LANES, 0),
            pl.BlockSpec((LANES, q_len), lambda b, c: (b * n_chunks + c, 0)),
            pl.BlockSpec((1, CONV_PAD, C_CONV_DIM), lambda b, c: (b, 0, 0)),
            const((CONV_PAD, C_CONV_DIM)), const((1, C_CONV_DIM)),
            const((1, LANES)), const((LANES, 1)), const((1, LANES)), const((LANES, 1)),
            const((1, C_D_INNER)), const((1, C_D_INNER)), const((LANES, C_D_INNER)),
            pl.BlockSpec((1, C_GROUPS, C_D_STATE, C_GW), lambda b, c: (b, 0, 0, 0)),
        ],
        out_specs=[
            pl.BlockSpec((q_len, C_D_INNER), lambda b, c: (b * n_chunks + c, 0)),
            pl.BlockSpec((1, C_GROUPS, C_D_STATE, C_GW), lambda b, c: (b, 0, 0, 0)),
        ],
        out_shape=[
            jax.ShapeDtypeStruct((rows, C_D_INNER), BF16),
            jax.ShapeDtypeStruct((batch, C_GROUPS, C_D_STATE, C_GW), F32),
        ],
        scratch_shapes=[
            pltpu.VMEM((q_len + CONV_PAD, C_CONV_DIM), F32),
            pltpu.VMEM((q_len, C_CONV_DIM), F32),
        ],
        compiler_params=_cparams(("parallel", "arbitrary")),
        name="ssd",
    )(proj, proj, proj, proj, dt_raw, dtrt, conv_state, cw, conv_b.reshape(1, -1),
      row(dt_bias), col(dt_bias), row(a_log), col(a_log),
      jnp.repeat(d_skip, C_HEAD_DIM).reshape(1, -1), norm_g.reshape(1, -1), expand, h0t)
    return y, h


def _state_to_t(h):
    b = h.shape[0]
    return h.reshape(b, C_GROUPS, C_HEADS // C_GROUPS, C_HEAD_DIM, C_D_STATE).transpose(0, 1, 4, 2, 3).reshape(
        b, C_GROUPS, C_D_STATE, C_GW)


def _state_from_t(ht):
    b = ht.shape[0]
    return ht.reshape(b, C_GROUPS, C_D_STATE, C_HEADS // C_GROUPS, C_HEAD_DIM).transpose(0, 1, 3, 4, 2).reshape(
        b, C_HEADS, C_HEAD_DIM, C_D_STATE)


def _router_kernel(x_ref, g_ref, w_ref, b_ref, base_ref, *rest, n_tiles):
    h_ref, mi_ref, mf_ref, cnt_ref = rest[-4:]
    i = pl.program_id(0)

    @pl.when(i == 0)
    def _():
        cnt_ref[...] = base_ref[...]

    @pl.when(i < n_tiles)
    def _():
        _route_tile(x_ref, g_ref, w_ref, b_ref, h_ref, mi_ref, mf_ref, cnt_ref)

    @pl.when(i >= n_tiles)
    def _():
        h_ref[...] = jnp.zeros_like(h_ref)


def _route_tile(x_ref, g_ref, w_ref, b_ref, h_ref, mi_ref, mf_ref, cnt_ref):
    tm = x_ref.shape[0]
    h = _rmsnorm_rows(x_ref[...], g_ref[...])
    h_ref[...] = h
    h_hi, h_lo, _ = _split3(h)
    w_hi, w_lo, _ = _split3(w_ref[...])
    logits = (_dot(h_lo, w_hi) + _dot(h_hi, w_lo)) + _dot(h_hi, w_hi) + b_ref[...]

    lane = lax.broadcasted_iota(jnp.int32, (tm, LANES), 1).astype(F32)
    big = float(4 * LANES)
    lg = jnp.where(lane < E_GROUPS, logits, -jnp.inf)
    mg = jnp.max(lg, axis=-1, keepdims=True)
    p_top = 1.0 / jnp.sum(jnp.exp(lg - mg), axis=-1, keepdims=True)
    g_top = jnp.min(jnp.where(lg == mg, lane, big), axis=-1, keepdims=True)
    lo = E_GROUPS + g_top * E_PER_GROUP
    le = jnp.where(jnp.logical_and(lane >= lo, lane < lo + E_PER_GROUP), logits, -jnp.inf)
    v1 = jnp.max(le, axis=-1, keepdims=True)
    i1 = jnp.min(jnp.where(le == v1, lane, big), axis=-1, keepdims=True)
    le2 = jnp.where(lane == i1, -jnp.inf, le)
    v2 = jnp.max(le2, axis=-1, keepdims=True)
    i2 = jnp.min(jnp.where(le2 == v2, lane, big), axis=-1, keepdims=True)
    e21 = jnp.exp(v2 - v1)
    gate1 = p_top / (1.0 + e21)
    gate2 = p_top * e21 / (1.0 + e21)
    eid1 = i1 - E_GROUPS
    eid2 = i2 - E_GROUPS

    oh1 = lane == eid1
    oh2 = lane == eid2
    oh = jnp.where(jnp.logical_or(oh1, oh2), 1.0, 0.0)
    ri = lax.broadcasted_iota(jnp.int32, (tm, tm), 0)
    ci = lax.broadcasted_iota(jnp.int32, (tm, tm), 1)
    before = jnp.where(ci < ri, 1.0, 0.0).astype(BF16)
    pos = _dot(before, oh.astype(BF16)) + cnt_ref[...]
    rank1 = jnp.sum(jnp.where(oh1, pos, 0.0), axis=-1, keepdims=True)
    rank2 = jnp.sum(jnp.where(oh2, pos, 0.0), axis=-1, keepdims=True)
    cnt_ref[...] = cnt_ref[...] + jnp.sum(oh, axis=0, keepdims=True)

    mi = jnp.where(lane == 0, eid1, jnp.where(lane == 1, eid2, jnp.where(lane == 2, rank1,
                                                                         jnp.where(lane == 3, rank2, 0.0))))
    sel = jnp.where(lax.broadcasted_iota(jnp.int32, (META_ROWS, LANES), 0)
                    == lax.broadcasted_iota(jnp.int32, (META_ROWS, LANES), 1), 1.0, 0.0).astype(BF16)
    mi_t = functools.reduce(jnp.add, [_dot_nt(sel, p) for p in _split3(mi)])
    mi_ref[...] = mi_t.astype(jnp.int32)
    mf_ref[...] = jnp.where(lane == 0, gate1, jnp.where(lane == 1, gate2, 0.0))


def _router(x, g, w_all, b_all, base, *, tm, h_into=None, h_row0=0):
    m = x.shape[0]
    n_tiles = m // tm
    last = n_tiles - 1
    tile = lambda i: jnp.minimum(i, last)
    one = pl.BlockSpec((1, LANES), lambda i: (0, 0))
    in_specs = [
        pl.BlockSpec((tm, D_MODEL), lambda i: (tile(i), 0)),
        pl.BlockSpec((1, D_MODEL), lambda i: (0, 0)),
        pl.BlockSpec((D_MODEL, LANES), lambda i: (0, 0)),
        one, one,
    ]
    args = [x, g.reshape(1, D_MODEL), w_all, b_all, base]
    if h_into is None:
        steps, h_rows, aliases = n_tiles + 1, (n_tiles + 1) * tm, {}
    else:
        steps, h_rows, aliases = n_tiles, h_into.shape[0], {len(args): 0}
        in_specs.append(pl.BlockSpec(memory_space=pl.ANY))
        args.append(h_into)
    return pl.pallas_call(
        functools.partial(_router_kernel, n_tiles=n_tiles),
        grid=(steps,),
        in_specs=in_specs,
        out_specs=[pl.BlockSpec((tm, D_MODEL), lambda i: (h_row0 // tm + i, 0)),
                   pl.BlockSpec((META_ROWS, tm), lambda i: (0, tile(i))),
                   pl.BlockSpec((tm, LANES), lambda i: (tile(i), 0)), one],
        out_shape=[
            jax.ShapeDtypeStruct((h_rows, D_MODEL), F32),
            jax.ShapeDtypeStruct((META_ROWS, m), jnp.int32),
            jax.ShapeDtypeStruct((m, LANES), F32),
            jax.ShapeDtypeStruct((1, LANES), F32),
        ],
        input_output_aliases=aliases,
        compiler_params=_cparams(("arbitrary",)),
        name="moe_router",
    )(*args)


def _gather_rows(idx_ref, base, n_rows, src_hbm, dst, sem):
    for r in range(n_rows):
        t = idx_ref[base + r]
        pltpu.make_async_copy(src_hbm.at[pl.ds(t, 1)], dst.at[pl.ds(r, 1)], sem).start()


def _wait_rows(n_rows, src_hbm, dst, sem):
    pltpu.make_async_copy(src_hbm.at[pl.ds(0, n_rows)], dst, sem).wait()


def _moe_ffn_kernel(be_ref, nu_ref, nv_ref, tok_ref, h_hbm, wg_ref, wu_ref, wd_ref, y_ref, xbuf, sem):
    i = pl.program_id(0)
    n_used = nu_ref[0]
    n_groups = MOE_ROWS // GATHER_ROWS

    def fetch(block):
        slot = block % 2
        for grp in range(n_groups):
            @pl.when(grp * GATHER_ROWS < nv_ref[block])
            def _():
                _gather_rows(tok_ref, block * MOE_ROWS + grp * GATHER_ROWS, GATHER_ROWS, h_hbm,
                             xbuf.at[slot, pl.ds(grp * GATHER_ROWS, GATHER_ROWS)], sem.at[slot, grp])

    @pl.when(i == 0)
    def _():
        xbuf[...] = jnp.zeros_like(xbuf)
        fetch(0)

    @pl.when(i + 1 < n_used)
    def _():
        fetch(i + 1)

    @pl.when(i < n_used)
    def _():
        slot = i % 2
        for grp in range(n_groups):
            @pl.when(grp * GATHER_ROWS < nv_ref[i])
            def _():
                _wait_rows(GATHER_ROWS, h_hbm, xbuf.at[slot, pl.ds(grp * GATHER_ROWS, GATHER_ROWS)],
                           sem.at[slot, grp])
        x = xbuf[slot].astype(BF16)
        gate = _dot(x, wg_ref[0, 0].astype(BF16))
        up = _dot(x, wu_ref[0, 0].astype(BF16))
        act = (gate * jax.nn.sigmoid(gate) * up).astype(BF16)
        y_ref[...] = _dot(act, wd_ref[0, 0].astype(BF16))

    @pl.when(i >= n_used)
    def _():
        y_ref[...] = jnp.zeros_like(y_ref)


def _moe_ffn(h_all, plan, w_gate, w_up, w_down, layer):
    blk_expert, n_used, n_valid, tok_of_slot = plan
    n_blocks = tok_of_slot.shape[0] // MOE_ROWS
    rows = lambda i, be, nu, nv, tok: (i, 0)
    wmap = lambda i, be, nu, nv, tok: (layer, be[i], 0, 0)
    return pl.pallas_call(
        _moe_ffn_kernel,
        grid_spec=pltpu.PrefetchScalarGridSpec(
            num_scalar_prefetch=4,
            grid=(n_blocks,),
            in_specs=[
                pl.BlockSpec(memory_space=pl.ANY),
                pl.BlockSpec((1, 1, D_MODEL, E_FF), wmap),
                pl.BlockSpec((1, 1, D_MODEL, E_FF), wmap),
                pl.BlockSpec((1, 1, E_FF, D_MODEL), wmap),
            ],
            out_specs=pl.BlockSpec((MOE_ROWS, D_MODEL), rows),
            scratch_shapes=[pltpu.VMEM((2, MOE_ROWS, D_MODEL), F32),
                            pltpu.SemaphoreType.DMA((2, MOE_ROWS // GATHER_ROWS))],
        ),
        out_shape=jax.ShapeDtypeStruct((n_blocks * MOE_ROWS, D_MODEL), F32),
        compiler_params=_cparams(("arbitrary",), unchecked_dma=True),
        name="moe_ffn",
    )(blk_expert, n_used, n_valid, tok_of_slot, h_all, w_gate, w_up, w_down)


def _moe_plan_kernel(meta_ref, cnt_ref, be_ref, nu_ref, nv_ref, tok_ref, dp_ref, ds_ref, start_scr,
                     *, n_p, n_s, n_blocks):
    n_tok = n_p + n_s

    def segment(e, first_blk):
        start_scr[e] = first_blk * MOE_ROWS
        cnt = cnt_ref[e]
        nb = (cnt + MOE_ROWS - 1) // MOE_ROWS

        def fill(j, carry):
            be_ref[first_blk + j] = e
            nv_ref[first_blk + j] = jnp.minimum(cnt - j * MOE_ROWS, MOE_ROWS)
            return carry
        lax.fori_loop(0, nb, fill, 0)
        return first_blk + nb
    n_used = lax.fori_loop(0, N_EXPERTS, segment, 0)
    nu_ref[0] = n_used

    last_expert = be_ref[n_used - 1]

    def tail(j, carry):
        be_ref[j] = last_expert
        nv_ref[j] = 0
        return carry
    lax.fori_loop(n_used, n_blocks, tail, 0)

    def clear(j, carry):
        tok_ref[j] = 0
        return carry
    lax.fori_loop(0, n_blocks * MOE_ROWS, clear, 0, unroll=8)

    def place(dest_ref, first, count):
        def body(j, carry):
            t = first + j
            for k in range(2):
                d = start_scr[meta_ref[k * n_tok + t]] + meta_ref[(2 + k) * n_tok + t]
                tok_ref[d] = t
                dest_ref[k * count + j] = d
            return carry
        lax.fori_loop(0, count, body, 0, unroll=4)
    place(dp_ref, 0, n_p)
    place(ds_ref, n_p, n_s)


def _moe_plan(meta, counts, *, n_p, n_s):
    n_blocks = (2 * (n_p + n_s)) // MOE_ROWS + N_EXPERTS
    smem = pl.BlockSpec(memory_space=pltpu.SMEM)
    i32 = lambda n: jax.ShapeDtypeStruct((n,), jnp.int32)
    be, nu, nv, tok, dp, ds = pl.pallas_call(
        functools.partial(_moe_plan_kernel, n_p=n_p, n_s=n_s, n_blocks=n_blocks),
        in_specs=[smem, smem],
        out_specs=[smem] * 6,
        out_shape=[i32(n_blocks), i32(1), i32(n_blocks), i32(n_blocks * MOE_ROWS), i32(2 * n_p), i32(2 * n_s)],
        scratch_shapes=[pltpu.SMEM((N_EXPERTS,), jnp.int32)],
        name="moe_plan",
    )(meta, counts)
    return (be, nu, nv, tok), dp, ds


def _moe_combine_kernel(dest_ref, x_ref, g_ref, y_hbm, o_ref, ybuf, sem, *, tm, n_tiles):
    i = pl.program_id(0)

    def fetch(tile):
        slot = tile % 2
        for k in range(2):
            _gather_rows(dest_ref, (k * n_tiles + tile) * tm, tm, y_hbm, ybuf.at[slot, k], sem.at[slot, k])

    @pl.when(i == 0)
    def _():
        fetch(0)

    @pl.when(i + 1 < n_tiles)
    def _():
        fetch(i + 1)

    slot = i % 2
    acc = x_ref[...]
    for k in range(2):
        _wait_rows(tm, y_hbm, ybuf.at[slot, k], sem.at[slot, k])
        acc = acc + g_ref[:, k:k + 1] * ybuf[slot, k]
    o_ref[...] = acc


def _moe_combine(x, gates, dest, y_rows, *, tm):
    m = x.shape[0]
    n_tiles = m // tm
    return pl.pallas_call(
        functools.partial(_moe_combine_kernel, tm=tm, n_tiles=n_tiles),
        grid_spec=pltpu.PrefetchScalarGridSpec(
            num_scalar_prefetch=1,
            grid=(n_tiles,),
            in_specs=[
                pl.BlockSpec((tm, D_MODEL), lambda i, d: (i, 0)),
                pl.BlockSpec((tm, LANES), lambda i, d: (i, 0)),
                pl.BlockSpec(memory_space=pl.ANY),
            ],
            out_specs=pl.BlockSpec((tm, D_MODEL), lambda i, d: (i, 0)),
            scratch_shapes=[pltpu.VMEM((2, 2, tm, D_MODEL), F32), pltpu.SemaphoreType.DMA((2, 2))],
        ),
        out_shape=jax.ShapeDtypeStruct((m, D_MODEL), F32),
        compiler_params=_cparams(("arbitrary",), unchecked_dma=True),
        name="moe_combine",
    )(dest, x, gates, y_rows)


def _moe(xp, xs, g, w_rg, b_rg, w_re, b_re, w_gate, w_up, w_down, layer):
    n_p, n_s = xp.shape[0], xs.shape[0]
    pad = LANES - E_GROUPS - N_EXPERTS
    w_all = jnp.pad(jnp.concatenate([w_rg, w_re.reshape(D_MODEL, N_EXPERTS)], axis=1), ((0, 0), (0, pad)))
    b_all = jnp.pad(jnp.concatenate([b_rg, b_re.reshape(N_EXPERTS)]), (0, pad)).reshape(1, LANES)
    h_all, mip, mfp, cnt_p = _router(xp, g, w_all, b_all, jnp.zeros((1, LANES), F32), tm=512)
    h_all, mis, mfs, cnt = _router(xs, g, w_all, b_all, cnt_p, tm=n_s, h_into=h_all, h_row0=n_p)
    meta = jnp.concatenate([mip[:4], mis[:4]], axis=1).reshape(-1)
    plan, dest_p, dest_s = _moe_plan(meta, cnt[0].astype(jnp.int32), n_p=n_p, n_s=n_s)
    y_rows = _moe_ffn(h_all, plan, w_gate, w_up, w_down, layer)
    return (_moe_combine(xp, mfp, dest_p, y_rows, tm=COMBINE_ROWS),
            _moe_combine(xs, mfs, dest_s, y_rows, tm=n_s))


def _final_norm_kernel(x_ref, g_ref, o_ref):
    o_ref[...] = _rmsnorm_rows(x_ref[...], g_ref[...])


def _final_norm(x, g, *, tm):
    m = x.shape[0]
    return pl.pallas_call(
        _final_norm_kernel,
        grid=(m // tm,),
        in_specs=[pl.BlockSpec((tm, D_MODEL), lambda i: (i, 0)), pl.BlockSpec((1, D_MODEL), lambda i: (0, 0))],
        out_specs=pl.BlockSpec((tm, D_MODEL), lambda i: (i, 0)),
        out_shape=jax.ShapeDtypeStruct((m, D_MODEL), F32),
        compiler_params=_cparams(("parallel",)),
        name="final_norm",
    )(x, g.reshape(1, D_MODEL))


def _rope_tables(pos):
    half = HEAD_DIM // 2
    inv = 1.0 / (ROPE_THETA ** (jnp.arange(half, dtype=F32) / half))
    ang = pos.astype(F32)[:, None] * inv[None, :]
    cos, sin = jnp.cos(ang), jnp.sin(ang)
    return jnp.concatenate([cos, cos], -1), jnp.concatenate([-sin, sin], -1)


def _kv_state(proj3, group, keep):
    b, t, _ = proj3.shape
    base = 2 * A_WIDTH + group * 3 * B_HW
    k = proj3[:, t - keep:, base + B_HW: base + 2 * B_HW]
    v = proj3[:, t - keep:, base + 2 * B_HW: base + 3 * B_HW]
    return jnp.stack([k, v], axis=2).reshape(b, keep, 2, B_HEADS, HEAD_DIM)


def _layer_ab(xp, xs, g, w_in_b, ln_g, ln_b, w_s, b_s, w_out_b, cache_views, layer, tabs_p, tabs_s, shapes):
    bp, tp, bs, ts = shapes

    proj_p = _inproj(xp, g, w_in_b, layer, tabs_p, tm=1024)
    a_p = _gmlp(proj_p, ln_g, ln_b, w_s, b_s, n_rows=bp * tp, chunks=2, emit_v=False)[0]
    outs, lses = zip(*[_attn_prompt(proj_p, grp, batch=bp, seq=tp) for grp in range(len(B_CONFIGS))])
    b_p = _merge(outs, lses, tm=1024)
    xp_new = _outproj([a_p, b_p], w_out_b, layer, xp, tm=1024, tn=1024)
    proj_p3 = proj_p.reshape(bp, tp, AB_IN)
    kv_p = [_kv_state(proj_p3, grp, min(w, tp)) for grp, (w, _) in enumerate(B_CONFIGS)]

    n_s = bs * ts
    proj_s = _inproj(xs, g, w_in_b, layer, tabs_s, tm=n_s)
    proj_s3 = proj_s.reshape(bs, ts, AB_IN)
    chunk_in = jnp.pad(proj_s3[:, :, :2 * A_WIDTH], ((0, 0), (0, A_CHUNK - ts), (0, 0)))
    a_s, v_s = _gmlp(chunk_in.reshape(bs * A_CHUNK, 2 * A_WIDTH), ln_g, ln_b, w_s, b_s,
                     n_rows=bs * A_CHUNK, chunks=1, emit_v=True)
    a_s = a_s.reshape(bs, A_CHUNK, A_WIDTH)[:, :ts].reshape(n_s, A_WIDTH)
    v_s = v_s.reshape(bs, A_CHUNK, A_WIDTH)[:, :ts]
    qkv_s = jnp.pad(proj_s3[:, :, 2 * A_WIDTH:], ((0, 0), (0, SAMPLE_ROWS - ts), (0, 0)))
    b_s_out = _attn_sample(qkv_s, cache_views, layer)[:, :ts].reshape(n_s, B_HW)
    xs_new = _outproj([a_s, b_s_out], w_out_b, layer, xs, tm=n_s, tn=1024)
    kv_s = [_kv_state(proj_s3, grp, ts) for grp in range(len(B_CONFIGS))]
    return xp_new, xs_new, kv_p, kv_s, v_s


def _layer_c(xp, xs, g, w_main, w_dt, conv_w, conv_b, dt_bias, a_log, d_skip, norm_g, w_out_b, layer,
             conv_state, ssm_state, shapes):
    bp, tp, bs, ts = shapes
    n_main = C_D_INNER + C_CONV_DIM
    args = (conv_w, conv_b, dt_bias, a_log, d_skip, norm_g)

    proj_p = _inproj(xp, g, w_main, layer, tm=1024, n_out=n_main)
    dt_p = _inproj(xp, g, w_dt, layer, tm=1024)
    zc = jnp.zeros((bp, CONV_PAD, C_CONV_DIM), F32)
    zh = jnp.zeros((bp, C_GROUPS, C_D_STATE, C_GW), F32)
    q_p = 128
    y_p, h_p = _ssd(proj_p, dt_p, zc, zh, *args, batch=bp, n_chunks=tp // q_p, q_len=q_p, t_valid=q_p)
    xp_new = _outproj([y_p], w_out_b, layer, xp, tm=512, tn=1024)
    conv_p = proj_p.reshape(bp, tp, n_main)[:, tp - (C_CONV - 1):, C_D_INNER:]

    n_s = bs * ts
    q_s = 16
    proj_s = _inproj(xs, g, w_main, layer, tm=n_s, n_out=n_main)
    dt_s = _inproj(xs, g, w_dt, layer, tm=n_s)
    pad_rows = lambda a: jnp.pad(a.reshape(bs, ts, -1), ((0, 0), (0, q_s - ts), (0, 0))).reshape(bs * q_s, -1)
    cst = jnp.pad(conv_state, ((0, 0), (CONV_PAD - (C_CONV - 1), 0), (0, 0)))
    y_s, h_s = _ssd(pad_rows(proj_s), pad_rows(dt_s), cst, _state_to_t(ssm_state), *args,
                    batch=bs, n_chunks=1, q_len=q_s, t_valid=ts)
    y_s = y_s.reshape(bs, q_s, C_D_INNER)[:, :ts].reshape(n_s, C_D_INNER)
    xs_new = _outproj([y_s], w_out_b, layer, xs, tm=n_s, tn=1024)
    assert ts >= C_CONV - 1
    conv_s = proj_s.reshape(bs, ts, n_main)[:, ts - (C_CONV - 1):, C_D_INNER:]
    return xp_new, xs_new, conv_p, conv_s, _state_from_t(h_p), _state_from_t(h_s)


def kernel(x_prompt, x_sample, cache_kv_w128, cache_kv_w512, cache_kv_w2048, state_conv, state_ssm,
           norm_mix, norm_ffn, norm_final, w_in_ab, a_ln_g, a_ln_b, a_w_s, a_b_s, w_out_ab,
           w_in_c, c_conv_w, c_conv_b, c_dt_bias, c_a_log, c_d, c_norm_g, w_out_c,
           w_router_g, b_router_g, w_router_e, b_router_e, w_exp_gate, w_exp_up, w_exp_down):
    bp, tp, _ = x_prompt.shape
    bs, ts, _ = x_sample.shape
    shapes = (bp, tp, bs, ts)
    depth = norm_mix.shape[0]
    kv_caches = (cache_kv_w128, cache_kv_w512, cache_kv_w2048)
    tabs_p = _rope_tables(jnp.arange(tp, dtype=jnp.int32))
    pos_s = PAST_LEN + jnp.arange(ts, dtype=jnp.int32)
    tabs_s = tuple(jnp.tile(t, (bs, 1)) for t in _rope_tables(pos_s))

    w_in_ab_b = w_in_ab.astype(BF16)
    w_out_ab_b = w_out_ab.astype(BF16)
    w_in_c_b = w_in_c.astype(BF16)
    w_dt_b = jnp.pad(w_in_c[:, :, C_D_INNER + C_CONV_DIM:], ((0, 0), (0, 0), (0, LANES - C_HEADS))).astype(BF16)
    w_out_c_b = w_out_c.astype(BF16)
    cache_views = _cache_views(kv_caches, ts)

    xp = x_prompt.reshape(bp * tp, D_MODEL)
    xs = x_sample.reshape(bs * ts, D_MODEL)
    kv_p = [[] for _ in B_CONFIGS]
    kv_s = [[] for _ in B_CONFIGS]
    chunk_v, conv_p, conv_s, ssm_p, ssm_s = [], [], [], [], []
    for l in range(depth):
        i = l // 2
        if l % 2 == 0:
            xp, xs, nkv_p, nkv_s, v_s = _layer_ab(
                xp, xs, norm_mix[l], w_in_ab_b, a_ln_g[i], a_ln_b[i], a_w_s[i], a_b_s[i], w_out_ab_b,
                cache_views, i, tabs_p, tabs_s, shapes)
            for grp in range(len(B_CONFIGS)):
                kv_p[grp].append(nkv_p[grp])
                kv_s[grp].append(nkv_s[grp])
            chunk_v.append(v_s)
        else:
            xp, xs, ncp, ncs, nsp, nss = _layer_c(
                xp, xs, norm_mix[l], w_in_c_b, w_dt_b, c_conv_w[i], c_conv_b[i], c_dt_bias[i], c_a_log[i], c_d[i],
                c_norm_g[i], w_out_c_b, i, state_conv[i], state_ssm[i], shapes)
            conv_p.append(ncp)
            conv_s.append(ncs)
            ssm_p.append(nsp)
            ssm_s.append(nss)
        xp, xs = _moe(xp, xs, norm_ffn[l], w_router_g[l], b_router_g[l], w_router_e[l], b_router_e[l],
                      w_exp_gate, w_exp_up, w_exp_down, l)
    y_p = _final_norm(xp, norm_final, tm=1024).reshape(bp, tp, D_MODEL)
    y_s = _final_norm(xs, norm_final, tm=bs * ts).reshape(bs, ts, D_MODEL)
    kv_s = [jnp.concatenate([c[:, :, ts:], jnp.stack(new)], axis=2) for c, new in zip(kv_caches, kv_s)]
    return (y_p, y_s,
            jnp.stack(kv_p[0]), kv_s[0],
            jnp.stack(kv_p[1]), kv_s[1],
            jnp.stack(kv_p[2]), kv_s[2],
            jnp.stack(chunk_v),
            jnp.stack(conv_p), jnp.stack(conv_s),
            jnp.stack(ssm_p), jnp.stack(ssm_s))
```

```python
import functools
import math

import jax
import jax.numpy as jnp
from jax import lax
from jax.experimental import pallas as pl
from jax.experimental.pallas import tpu as pltpu

F32 = jnp.float32
BF16 = jnp.bfloat16

D_MODEL = 2048
PAST_LEN = 16384
EPS = 1e-6
A_WIDTH = 1024
A_GROUPS = 8
A_GW = 128
A_CHUNK = 128
HEAD_DIM = 128
B_HEADS = 4
B_CONFIGS = ((128, 1), (512, 4), (2048, 16))
B_BAND = 128
B_HW = B_HEADS * HEAD_DIM
AB_IN = 2 * A_WIDTH + 3 * len(B_CONFIGS) * B_HW
ROPE_THETA = 10000.0
C_D_INNER = 4096
C_HEAD_DIM = 64
C_HEADS = 64
C_GROUPS = 8
C_D_STATE = 128
C_CONV = 4
C_CONV_DIM = C_D_INNER + 2 * C_GROUPS * C_D_STATE
C_GW = C_D_INNER // C_GROUPS
E_GROUPS = 8
E_PER_GROUP = 8
N_EXPERTS = 64
E_FF = 512
MOE_ROWS = 256
GATHER_ROWS = 32
COMBINE_ROWS = 128
META_ROWS = 8

LANES = 128
VMEM_LIMIT = 56 * 1024 * 1024


def _cparams(sem, unchecked_dma=False):
    return pltpu.CompilerParams(dimension_semantics=sem, vmem_limit_bytes=VMEM_LIMIT,
                                disable_bounds_checks=unchecked_dma)


def _split3(x):
    hi = x.astype(BF16)
    r1 = x - hi.astype(F32)
    mid = r1.astype(BF16)
    lo = (r1 - mid.astype(F32)).astype(BF16)
    return hi, mid, lo


def _dot(a, b):
    return jnp.dot(a, b, preferred_element_type=F32)


def _dot_nt(a, b):
    return lax.dot_general(a, b, (((1,), (1,)), ((), ())), preferred_element_type=F32)


def _rmsnorm_rows(x, g):
    ms = jnp.mean(x * x, axis=-1, keepdims=True)
    return x * lax.rsqrt(ms + EPS) * g


def _inproj_kernel(x_ref, g_ref, w_ref, *rest, rope):
    o_ref, h_scr = rest[-2:]
    j = pl.program_id(1)

    @pl.when(j == 0)
    def _():
        h_scr[...] = _rmsnorm_rows(x_ref[...], g_ref[...]).astype(BF16)

    acc = _dot(h_scr[...], w_ref[0])
    if not rope:
        o_ref[...] = acc
        return
    cos_ref, sin_ref = rest[:2]
    is_qk = jnp.logical_and(j >= 4, (j - 4) % 3 != 2)

    @pl.when(is_qk)
    def _():
        cos = cos_ref[...]
        sin = sin_ref[...]
        for h in range(B_HEADS):
            sl = slice(h * HEAD_DIM, (h + 1) * HEAD_DIM)
            a = acc[:, sl]
            o_ref[:, sl] = a * cos + pltpu.roll(a, HEAD_DIM // 2, 1) * sin

    @pl.when(jnp.logical_not(is_qk))
    def _():
        o_ref[...] = acc


def _inproj(x, g, w_bf16, layer, tabs=None, *, tm, n_out=None):
    m, _ = x.shape
    n = n_out or w_bf16.shape[2]
    tn = B_HW if n % B_HW == 0 else n
    in_specs = [
        pl.BlockSpec((tm, D_MODEL), lambda i, j: (i, 0)),
        pl.BlockSpec((1, D_MODEL), lambda i, j: (0, 0)),
        pl.BlockSpec((1, D_MODEL, tn), lambda i, j: (layer, 0, j)),
    ]
    args = [x, g.reshape(1, D_MODEL), w_bf16]
    if tabs is not None:
        n_pos_tiles = tabs[0].shape[0] // tm
        in_specs += [pl.BlockSpec((tm, HEAD_DIM), lambda i, j: (i % n_pos_tiles, 0))] * 2
        args += list(tabs)
    return pl.pallas_call(
        functools.partial(_inproj_kernel, rope=tabs is not None),
        grid=(m // tm, n // tn),
        in_specs=in_specs,
        out_specs=pl.BlockSpec((tm, tn), lambda i, j: (i, j)),
        out_shape=jax.ShapeDtypeStruct((m, n), F32),
        scratch_shapes=[pltpu.VMEM((tm, D_MODEL), BF16)],
        compiler_params=_cparams(("parallel", "arbitrary")),
        name="inproj",
    )(*args)


def _gmlp_kernel(p_ref, lng_ref, lnb_ref, ws_ref, bs_ref, a_ref, *v_ref, chunks):
    row = lax.broadcasted_iota(jnp.int32, (A_CHUNK, A_CHUNK), 0)
    col = lax.broadcasted_iota(jnp.int32, (A_CHUNK, A_CHUNK), 1)
    causal = col <= row
    for c in range(chunks):
        rows = slice(c * A_CHUNK, (c + 1) * A_CHUNK)
        u = jax.nn.gelu(p_ref[rows, 0:A_WIDTH], approximate=True)
        vg = jax.nn.gelu(p_ref[rows, A_WIDTH:2 * A_WIDTH], approximate=True)
        vc = vg - jnp.mean(vg, axis=-1, keepdims=True)
        v = vc * lax.rsqrt(jnp.mean(vc * vc, axis=-1, keepdims=True) + EPS)
        v = v * lng_ref[...] + lnb_ref[...]
        if v_ref:
            v_ref[0][rows, :] = v
        vb = v.astype(BF16)
        for g in range(A_GROUPS):
            sl = slice(g * A_GW, (g + 1) * A_GW)
            w = jnp.where(causal, ws_ref[g], 0.0).astype(BF16)
            mixed = _dot(w, vb[:, sl]) + bs_ref[:, sl]
            a_ref[rows, sl] = (u[:, sl] * mixed).astype(BF16)


def _gmlp(proj, ln_g, ln_b, w_s, b_s, *, n_rows, chunks, emit_v):
    rows = chunks * A_CHUNK
    bsb = jnp.repeat(jnp.transpose(b_s), A_GW, axis=1)
    out_shape = [jax.ShapeDtypeStruct((n_rows, A_WIDTH), BF16)]
    out_specs = [pl.BlockSpec((rows, A_WIDTH), lambda i: (i, 0))]
    if emit_v:
        out_shape.append(jax.ShapeDtypeStruct((n_rows, A_WIDTH), F32))
        out_specs.append(pl.BlockSpec((rows, A_WIDTH), lambda i: (i, 0)))
    return pl.pallas_call(
        functools.partial(_gmlp_kernel, chunks=chunks),
        grid=(n_rows // rows,),
        in_specs=[
            pl.BlockSpec((rows, 2 * A_WIDTH), lambda i: (i, 0)),
            pl.BlockSpec((1, A_WIDTH), lambda i: (0, 0)),
            pl.BlockSpec((1, A_WIDTH), lambda i: (0, 0)),
            pl.BlockSpec((A_GROUPS, A_CHUNK, A_CHUNK), lambda i: (0, 0, 0)),
            pl.BlockSpec((A_CHUNK, A_WIDTH), lambda i: (0, 0)),
        ],
        out_specs=out_specs,
        out_shape=out_shape,
        compiler_params=_cparams(("parallel",)),
        name="gmlp",
    )(proj, ln_g.reshape(1, A_WIDTH), ln_b.reshape(1, A_WIDTH), w_s, bsb)


def _attn_prompt_kernel(q_ref, kp_ref, kc_ref, vp_ref, vc_ref, o_ref, lse_ref, *, dil, heads):
    n = pl.program_id(2)
    qi = lax.broadcasted_iota(jnp.int32, (B_BAND, B_BAND), 0)
    kj = lax.broadcasted_iota(jnp.int32, (B_BAND, B_BAND), 1)
    prev_ok = jnp.logical_and(kj >= qi, n > 0)
    cur_ok = kj <= qi
    scale = HEAD_DIM ** -0.5
    for r in range(dil):
        rows = pl.ds(r, B_BAND, stride=dil) if dil > 1 else slice(None)
        for h in range(heads):
            sl = slice(h * HEAD_DIM, (h + 1) * HEAD_DIM)
            q = q_ref[rows, sl].astype(BF16)
            sp = _dot_nt(q, kp_ref[rows, sl].astype(BF16)) * scale
            sc = _dot_nt(q, kc_ref[rows, sl].astype(BF16)) * scale
            sp = jnp.where(prev_ok, sp, -jnp.inf)
            sc = jnp.where(cur_ok, sc, -jnp.inf)
            m = jnp.maximum(jnp.max(sp, axis=-1, keepdims=True), jnp.max(sc, axis=-1, keepdims=True))
            pp = jnp.exp(sp - m)
            pc = jnp.exp(sc - m)
            l = jnp.sum(pp, axis=-1, keepdims=True) + jnp.sum(pc, axis=-1, keepdims=True)
            o = (_dot(pp.astype(BF16), vp_ref[rows, sl].astype(BF16))
                 + _dot(pc.astype(BF16), vc_ref[rows, sl].astype(BF16)))
            o_ref[rows, sl] = o / l
            lse_ref[rows, sl] = jnp.broadcast_to(m + jnp.log(l), (B_BAND, HEAD_DIM))


def _attn_prompt(proj, group, *, batch, seq):
    dil = B_CONFIGS[group][1]
    rows = B_BAND * dil
    nblk = seq // rows
    heads = B_HEADS if dil == 1 else 1
    width = heads * HEAD_DIM
    n_col = B_HW // width
    qcol = (4 + 3 * group) * n_col

    def spec(off, prev):
        def imap(b, h, n):
            nn = jnp.maximum(n - 1, 0) if prev else n
            return (b * nblk + nn, qcol + off * n_col + h)
        return pl.BlockSpec((rows, width), imap)

    out_spec = pl.BlockSpec((rows, width), lambda b, h, n: (b * nblk + n, h))
    return pl.pallas_call(
        functools.partial(_attn_prompt_kernel, dil=dil, heads=heads),
        grid=(batch, n_col, nblk),
        in_specs=[spec(0, False), spec(1, True), spec(1, False), spec(2, True), spec(2, False)],
        out_specs=[out_spec, out_spec],
        out_shape=[jax.ShapeDtypeStruct((batch * seq, B_HW), F32)] * 2,
        compiler_params=_cparams(("parallel", "parallel", "arbitrary")),
        name=f"attn_prompt_g{group}",
    )(proj, proj, proj, proj, proj)


def _merge_kernel(o0, o1, o2, l0, l1, l2, b_ref):
    m = jnp.maximum(jnp.maximum(l0[...], l1[...]), l2[...])
    e0 = jnp.exp(l0[...] - m)
    e1 = jnp.exp(l1[...] - m)
    e2 = jnp.exp(l2[...] - m)
    den = e0 + e1 + e2
    b_ref[...] = ((e0 / den) * o0[...] + (e1 / den) * o1[...] + (e2 / den) * o2[...]).astype(BF16)


def _merge(outs, lses, *, tm):
    m = outs[0].shape[0]
    spec = pl.BlockSpec((tm, B_HW), lambda i: (i, 0))
    return pl.pallas_call(
        _merge_kernel,
        grid=(m // tm,),
        in_specs=[spec] * 6,
        out_specs=spec,
        out_shape=jax.ShapeDtypeStruct((m, B_HW), BF16),
        compiler_params=_cparams(("parallel",)),
        name="attn_merge",
    )(*outs, *lses)


SAMPLE_ROWS = 16


def _attn_sample_kernel(qkv_ref, c0_ref, c1_ref, c2_ref, b_ref, *, n_real):
    caches = (c0_ref, c1_ref, c2_ref)
    row_w = 2 * B_HW
    scale = HEAD_DIM ** -0.5
    qi_c = lax.broadcasted_iota(jnp.int32, (SAMPLE_ROWS, B_BAND), 0)
    ka_c = lax.broadcasted_iota(jnp.int32, (SAMPLE_ROWS, B_BAND), 1)
    qi_n = lax.broadcasted_iota(jnp.int32, (SAMPLE_ROWS, SAMPLE_ROWS), 0)
    km_n = lax.broadcasted_iota(jnp.int32, (SAMPLE_ROWS, SAMPLE_ROWS), 1)

    def valid(diff, dil):
        shift = dil.bit_length() - 1
        ok = jnp.logical_and(diff >= 0, (diff & (dil - 1)) == 0)
        return jnp.logical_and(ok, (diff >> shift) <= B_BAND)

    for h in range(B_HEADS):
        hs = slice(h * HEAD_DIM, (h + 1) * HEAD_DIM)
        outs, lses = [], []
        for g, (window, dil) in enumerate(B_CONFIGS):
            base = g * 3 * B_HW
            q = qkv_ref[0, :, base + h * HEAD_DIM: base + (h + 1) * HEAD_DIM].astype(BF16)
            kn = qkv_ref[0, :, base + B_HW + h * HEAD_DIM: base + B_HW + (h + 1) * HEAD_DIM].astype(BF16)
            vn = qkv_ref[0, :, base + 2 * B_HW + h * HEAD_DIM: base + 2 * B_HW + (h + 1) * HEAD_DIM].astype(BF16)
            n_res = min(dil, n_real)
            scores, values = [], []
            for r in range(n_res):
                kc = caches[g][0, 0, :, r * row_w + h * HEAD_DIM: r * row_w + (h + 1) * HEAD_DIM].astype(BF16)
                vc = caches[g][0, 0, :, r * row_w + B_HW + h * HEAD_DIM: r * row_w + B_HW + (h + 1) * HEAD_DIM].astype(BF16)
                s = _dot_nt(q, kc) * scale
                diff = qi_c + window - ka_c * dil - r
                scores.append(jnp.where(valid(diff, dil), s, -jnp.inf))
                values.append(vc)
            s = _dot_nt(q, kn) * scale
            scores.append(jnp.where(valid(qi_n - km_n, dil), s, -jnp.inf))
            values.append(vn)
            m = functools.reduce(jnp.maximum, [jnp.max(s, axis=-1, keepdims=True) for s in scores])
            ps = [jnp.exp(s - m) for s in scores]
            l = functools.reduce(jnp.add, [jnp.sum(p, axis=-1, keepdims=True) for p in ps])
            o = functools.reduce(jnp.add, [_dot(p.astype(BF16), v) for p, v in zip(ps, values)])
            outs.append(o / l)
            lses.append(m + jnp.log(l))
        m = functools.reduce(jnp.maximum, lses)
        es = [jnp.exp(l - m) for l in lses]
        den = functools.reduce(jnp.add, es)
        b_ref[0, :, hs] = functools.reduce(jnp.add, [(e / den) * o for e, o in zip(es, outs)]).astype(BF16)


def _cache_views(caches, n_real):
    views = []
    for (window, dil), c in zip(B_CONFIGS, caches):
        layers, b = c.shape[:2]
        assert c.shape[2] == window and window == dil * B_BAND
        n_res = min(dil, n_real)
        c = c.reshape(layers, b, B_BAND, dil, 2, B_HEADS, HEAD_DIM)[:, :, :, :n_res]
        views.append(c.reshape(layers, b, B_BAND, n_res * 2 * B_HW))
    return views


def _attn_sample(qkv, views, layer):
    b = qkv.shape[0]
    n_real = max(v.shape[3] // (2 * B_HW) for v in views)
    specs = [pl.BlockSpec((1, 1) + v.shape[2:], lambda i: (layer, i, 0, 0)) for v in views]
    return pl.pallas_call(
        functools.partial(_attn_sample_kernel, n_real=n_real),
        grid=(b,),
        in_specs=[pl.BlockSpec((1, SAMPLE_ROWS, qkv.shape[2]), lambda i: (i, 0, 0))] + specs,
        out_specs=pl.BlockSpec((1, SAMPLE_ROWS, B_HW), lambda i: (i, 0, 0)),
        out_shape=jax.ShapeDtypeStruct((b, SAMPLE_ROWS, B_HW), BF16),
        compiler_params=_cparams(("parallel",)),
        name="attn_sample",
    )(qkv, *views)


def _outproj_kernel(*refs, n_parts):
    a_refs = refs[:n_parts]
    w_refs = refs[n_parts:2 * n_parts]
    x_ref, o_ref = refs[2 * n_parts], refs[2 * n_parts + 1]
    acc = x_ref[...]
    for a, w in zip(a_refs, w_refs):
        acc = acc + _dot(a[...], w[0])
    o_ref[...] = acc


def _outproj(parts, w_bf16, layer, x, *, tm, tn):
    m, n = x.shape
    in_specs = [pl.BlockSpec((tm, a.shape[1]), lambda j, i: (i, 0)) for a in parts]
    row0 = 0
    for a in parts:
        width = a.shape[1]
        blk = row0 // width
        assert blk * width == row0
        in_specs.append(pl.BlockSpec((1, width, tn), lambda j, i, blk=blk: (layer, blk, j)))
        row0 += width
    assert row0 == w_bf16.shape[1]
    weights = [w_bf16] * len(parts)
    in_specs.append(pl.BlockSpec((tm, tn), lambda j, i: (i, j)))
    return pl.pallas_call(
        functools.partial(_outproj_kernel, n_parts=len(parts)),
        grid=(n // tn, m // tm),
        in_specs=in_specs,
        out_specs=pl.BlockSpec((tm, tn), lambda j, i: (i, j)),
        out_shape=jax.ShapeDtypeStruct((m, n), F32),
        compiler_params=_cparams(("parallel", "parallel")),
        name="outproj",
    )(*parts, *weights, x)


CONV_PAD = 8


def _ssd_kernel(z_ref, xa_ref, xb_ref, bc_ref, dtr_ref, dtrt_ref, cst_ref, cw_ref, cb_ref,
                dtb_ref, dtbc_ref, alog_ref, alogc_ref, dsk_ref, ng_ref, e_ref, h0_ref,
                y_ref, h_ref, ext_scr, act_scr, *, q_len, t_valid):
    c = pl.program_id(1)

    @pl.when(c == 0)
    def _():
        ext_scr[0:CONV_PAD, :] = cst_ref[0]
        h_ref[...] = h0_ref[...]

    half = C_CONV_DIM // 3
    ext_scr[CONV_PAD:CONV_PAD + q_len, 0:half] = xa_ref[...]
    ext_scr[CONV_PAD:CONV_PAD + q_len, half:2 * half] = xb_ref[...]
    ext_scr[CONV_PAD:CONV_PAD + q_len, 2 * half:3 * half] = bc_ref[...]
    for s in range(C_CONV_DIM // C_GW):
        sl = slice(s * C_GW, (s + 1) * C_GW)
        acc = cb_ref[:, sl]
        for j in range(C_CONV):
            lo = CONV_PAD - (C_CONV - 1) + j
            acc = acc + ext_scr[lo:lo + q_len, sl] * cw_ref[j:j + 1, sl]
        act_scr[:, sl] = acc * jax.nn.sigmoid(acc)
    ext_scr[0:CONV_PAD, :] = ext_scr[q_len:q_len + CONV_PAD, :]

    ti = lax.broadcasted_iota(jnp.int32, (q_len, q_len), 0)
    si = lax.broadcasted_iota(jnp.int32, (q_len, q_len), 1)
    tri = si <= ti
    tri_b = jnp.where(tri, 1.0, 0.0).astype(BF16)
    tri_tb = jnp.where(ti <= si, 1.0, 0.0).astype(BF16)

    dt = jax.nn.softplus(dtr_ref[...] + dtb_ref[...])
    dtt = jax.nn.softplus(dtrt_ref[...] + dtbc_ref[...])
    if t_valid < q_len:
        dt = jnp.where(lax.broadcasted_iota(jnp.int32, dt.shape, 0) < t_valid, dt, 0.0)
        dtt = jnp.where(lax.broadcasted_iota(jnp.int32, dtt.shape, 1) < t_valid, dtt, 0.0)
    da = dt * (-jnp.exp(alog_ref[...]))
    dat = dtt * (-jnp.exp(alogc_ref[...]))
    a_cum = functools.reduce(jnp.add, [_dot(tri_b, p) for p in reversed(_split3(da))])
    a_cumt = functools.reduce(jnp.add, [_dot(p, tri_tb) for p in reversed(_split3(dat))])
    dt_parts = _split3(dt)
    ac_parts = _split3(a_cum)

    lane_lo = lax.broadcasted_iota(jnp.int32, (q_len, LANES), 1) < C_HEAD_DIM

    for g in range(C_GROUPS):
        gs = slice(g * C_GW, (g + 1) * C_GW)
        e_g = e_ref[:, gs]
        dt_x = functools.reduce(jnp.add, [_dot(p, e_g) for p in reversed(dt_parts)])
        ac_x = functools.reduce(jnp.add, [_dot(p, e_g) for p in reversed(ac_parts)])
        xs = act_scr[:, gs]
        bm = act_scr[:, C_D_INNER + g * C_D_STATE: C_D_INNER + (g + 1) * C_D_STATE]
        cm = act_scr[:, C_D_INNER + C_GROUPS * C_D_STATE + g * C_D_STATE:
                     C_D_INNER + C_GROUPS * C_D_STATE + (g + 1) * C_D_STATE]
        bm_b = bm.astype(BF16)
        cm_b = cm.astype(BF16)
        xdt = xs * dt_x
        a_last = ac_x[q_len - 1:q_len, :]
        xdt_b = xdt.astype(BF16)
        xdt_end_b = (xdt * jnp.exp(a_last - ac_x)).astype(BF16)
        cbm = _dot_nt(cm_b, bm_b)

        h_prev = h_ref[0, g]
        y = _dot(cm_b, h_prev.astype(BF16)) * jnp.exp(ac_x)
        st = _dot(jnp.transpose(bm).astype(BF16), xdt_end_b)
        h_ref[0, g] = h_prev * jnp.exp(a_last) + st

        pairs = []
        for hp in range(C_GW // LANES):
            ms = []
            for hh in (2 * hp, 2 * hp + 1):
                hd = g * (C_GW // C_HEAD_DIM) + hh
                seg = a_cum[:, hd:hd + 1] - a_cumt[hd:hd + 1, :]
                decay = jnp.exp(jnp.where(tri, seg, -jnp.inf))
                ms.append((cbm * decay).astype(BF16))
            slab = xdt_b[:, hp * LANES:(hp + 1) * LANES]
            zero = jnp.zeros_like(slab)
            rhs = jnp.concatenate([jnp.where(lane_lo, slab, zero), jnp.where(lane_lo, zero, slab)], axis=0)
            pairs.append(_dot(jnp.concatenate(ms, axis=1), rhs))
        y = y + jnp.concatenate(pairs, axis=1)
        y = y + xs * dsk_ref[:, gs]
        zz = z_ref[:, gs]
        y = y * (zz * jax.nn.sigmoid(zz))
        y = y * lax.rsqrt(jnp.mean(y * y, axis=-1, keepdims=True) + EPS) * ng_ref[:, gs]
        y_ref[:, gs] = y.astype(BF16)


def _ssd(proj, dt_raw, conv_state, h0t, conv_w, conv_b, dt_bias, a_log, d_skip, norm_g,
         *, batch, n_chunks, q_len, t_valid):
    rows = batch * n_chunks * q_len
    dtrt = dt_raw.reshape(batch * n_chunks, q_len, LANES).transpose(0, 2, 1).reshape(-1, q_len)
    pad = LANES - C_HEADS
    col = lambda v: jnp.pad(v, (0, pad)).reshape(LANES, 1)
    row = lambda v: jnp.pad(v, (0, pad)).reshape(1, LANES)
    expand = jnp.repeat(jnp.eye(LANES, C_HEADS, dtype=BF16), C_HEAD_DIM, axis=1)
    cw = jnp.pad(conv_w, ((0, CONV_PAD - C_CONV), (0, 0)))
    xcol = C_D_INNER // (C_CONV_DIM // 3)
    const = lambda shape: pl.BlockSpec(shape, lambda b, c: (0,) * len(shape))
    blk = lambda width, j: pl.BlockSpec((q_len, width), lambda b, c: (b * n_chunks + c, j))
    y, h = pl.pallas_call(
        functools.partial(_ssd_kernel, q_len=q_len, t_valid=t_valid),
        grid=(batch, n_chunks),
        in_specs=[
            blk(C_D_INNER, 0), blk(C_CONV_DIM // 3, xcol), blk(C_CONV_DIM // 3, xcol + 1),
            blk(C_CONV_DIM // 3, xcol + 2), blk(LANES, 0),
            pl.BlockSpec((LANES, q_len), lambda b, c: (b * n_chunks + c, 0)),
            pl.BlockSpec((1, CONV_PAD, C_CONV_DIM), lambda b, c: (b, 0, 0)),
            const((CONV_PAD, C_CONV_DIM)), const((1, C_CONV_DIM)),
            const((1, LANES)), const((LANES, 1)), const((1, LANES)), const((LANES, 1)),
            const((1, C_D_INNER)), const((1, C_D_INNER)), const((LANES, C_D_INNER)),
            pl.BlockSpec((1, C_GROUPS, C_D_STATE, C_GW), lambda b, c: (b, 0, 0, 0)),
        ],
        out_specs=[
            pl.BlockSpec((q_len, C_D_INNER), lambda b, c: (b * n_chunks + c, 0)),
            pl.BlockSpec((1, C_GROUPS, C_D_STATE, C_GW), lambda b, c: (b, 0, 0, 0)),
        ],
        out_shape=[
            jax.ShapeDtypeStruct((rows, C_D_INNER), BF16),
            jax.ShapeDtypeStruct((batch, C_GROUPS, C_D_STATE, C_GW), F32),
        ],
        scratch_shapes=[
            pltpu.VMEM((q_len + CONV_PAD, C_CONV_DIM), F32),
            pltpu.VMEM((q_len, C_CONV_DIM), F32),
        ],
        compiler_params=_cparams(("parallel", "arbitrary")),
        name="ssd",
    )(proj, proj, proj, proj, dt_raw, dtrt, conv_state, cw, conv_b.reshape(1, -1),
      row(dt_bias), col(dt_bias), row(a_log), col(a_log),
      jnp.repeat(d_skip, C_HEAD_DIM).reshape(1, -1), norm_g.reshape(1, -1), expand, h0t)
    return y, h


def _state_to_t(h):
    b = h.shape[0]
    return h.reshape(b, C_GROUPS, C_HEADS // C_GROUPS, C_HEAD_DIM, C_D_STATE).transpose(0, 1, 4, 2, 3).reshape(
        b, C_GROUPS, C_D_STATE, C_GW)


def _state_from_t(ht):
    b = ht.shape[0]
    return ht.reshape(b, C_GROUPS, C_D_STATE, C_HEADS // C_GROUPS, C_HEAD_DIM).transpose(0, 1, 3, 4, 2).reshape(
        b, C_HEADS, C_HEAD_DIM, C_D_STATE)


def _router_kernel(x_ref, g_ref, w_ref, b_ref, base_ref, *rest, n_tiles):
    h_ref, mi_ref, mf_ref, cnt_ref = rest[-4:]
    i = pl.program_id(0)

    @pl.when(i == 0)
    def _():
        cnt_ref[...] = base_ref[...]

    @pl.when(i < n_tiles)
    def _():
        _route_tile(x_ref, g_ref, w_ref, b_ref, h_ref, mi_ref, mf_ref, cnt_ref)

    @pl.when(i >= n_tiles)
    def _():
        h_ref[...] = jnp.zeros_like(h_ref)


def _route_tile(x_ref, g_ref, w_ref, b_ref, h_ref, mi_ref, mf_ref, cnt_ref):
    tm = x_ref.shape[0]
    h = _rmsnorm_rows(x_ref[...], g_ref[...])
    h_ref[...] = h
    h_hi, h_lo, _ = _split3(h)
    w_hi, w_lo, _ = _split3(w_ref[...])
    logits = (_dot(h_lo, w_hi) + _dot(h_hi, w_lo)) + _dot(h_hi, w_hi) + b_ref[...]

    lane = lax.broadcasted_iota(jnp.int32, (tm, LANES), 1).astype(F32)
    big = float(4 * LANES)
    lg = jnp.where(lane < E_GROUPS, logits, -jnp.inf)
    mg = jnp.max(lg, axis=-1, keepdims=True)
    p_top = 1.0 / jnp.sum(jnp.exp(lg - mg), axis=-1, keepdims=True)
    g_top = jnp.min(jnp.where(lg == mg, lane, big), axis=-1, keepdims=True)
    lo = E_GROUPS + g_top * E_PER_GROUP
    le = jnp.where(jnp.logical_and(lane >= lo, lane < lo + E_PER_GROUP), logits, -jnp.inf)
    v1 = jnp.max(le, axis=-1, keepdims=True)
    i1 = jnp.min(jnp.where(le == v1, lane, big), axis=-1, keepdims=True)
    le2 = jnp.where(lane == i1, -jnp.inf, le)
    v2 = jnp.max(le2, axis=-1, keepdims=True)
    i2 = jnp.min(jnp.where(le2 == v2, lane, big), axis=-1, keepdims=True)
    e21 = jnp.exp(v2 - v1)
    gate1 = p_top / (1.0 + e21)
    gate2 = p_top * e21 / (1.0 + e21)
    eid1 = i1 - E_GROUPS
    eid2 = i2 - E_GROUPS

    oh1 = lane == eid1
    oh2 = lane == eid2
    oh = jnp.where(jnp.logical_or(oh1, oh2), 1.0, 0.0)
    ri = lax.broadcasted_iota(jnp.int32, (tm, tm), 0)
    ci = lax.broadcasted_iota(jnp.int32, (tm, tm), 1)
    before = jnp.where(ci < ri, 1.0, 0.0).astype(BF16)
    pos = _dot(before, oh.astype(BF16)) + cnt_ref[...]
    rank1 = jnp.sum(jnp.where(oh1, pos, 0.0), axis=-1, keepdims=True)
    rank2 = jnp.sum(jnp.where(oh2, pos, 0.0), axis=-1, keepdims=True)
    cnt_ref[...] = cnt_ref[...] + jnp.sum(oh, axis=0, keepdims=True)

    mi = jnp.where(lane == 0, eid1, jnp.where(lane == 1, eid2, jnp.where(lane == 2, rank1,
                                                                         jnp.where(lane == 3, rank2, 0.0))))
    sel = jnp.where(lax.broadcasted_iota(jnp.int32, (META_ROWS, LANES), 0)
                    == lax.broadcasted_iota(jnp.int32, (META_ROWS, LANES), 1), 1.0, 0.0).astype(BF16)
    mi_t = functools.reduce(jnp.add, [_dot_nt(sel, p) for p in _split3(mi)])
    mi_ref[...] = mi_t.astype(jnp.int32)
    mf_ref[...] = jnp.where(lane == 0, gate1, jnp.where(lane == 1, gate2, 0.0))


def _router(x, g, w_all, b_all, base, *, tm, h_into=None, h_row0=0):
    m = x.shape[0]
    n_tiles = m // tm
    last = n_tiles - 1
    tile = lambda i: jnp.minimum(i, last)
    one = pl.BlockSpec((1, LANES), lambda i: (0, 0))
    in_specs = [
        pl.BlockSpec((tm, D_MODEL), lambda i: (tile(i), 0)),
        pl.BlockSpec((1, D_MODEL), lambda i: (0, 0)),
        pl.BlockSpec((D_MODEL, LANES), lambda i: (0, 0)),
        one, one,
    ]
    args = [x, g.reshape(1, D_MODEL), w_all, b_all, base]
    if h_into is None:
        steps, h_rows, aliases = n_tiles + 1, (n_tiles + 1) * tm, {}
    else:
        steps, h_rows, aliases = n_tiles, h_into.shape[0], {len(args): 0}
        in_specs.append(pl.BlockSpec(memory_space=pl.ANY))
        args.append(h_into)
    return pl.pallas_call(
        functools.partial(_router_kernel, n_tiles=n_tiles),
        grid=(steps,),
        in_specs=in_specs,
        out_specs=[pl.BlockSpec((tm, D_MODEL), lambda i: (h_row0 // tm + i, 0)),
                   pl.BlockSpec((META_ROWS, tm), lambda i: (0, tile(i))),
                   pl.BlockSpec((tm, LANES), lambda i: (tile(i), 0)), one],
        out_shape=[
            jax.ShapeDtypeStruct((h_rows, D_MODEL), F32),
            jax.ShapeDtypeStruct((META_ROWS, m), jnp.int32),
            jax.ShapeDtypeStruct((m, LANES), F32),
            jax.ShapeDtypeStruct((1, LANES), F32),
        ],
        input_output_aliases=aliases,
        compiler_params=_cparams(("arbitrary",)),
        name="moe_router",
    )(*args)


def _gather_rows(idx_ref, base, n_rows, src_hbm, dst, sem, priority=0):
    for r in range(n_rows):
        t = idx_ref[base + r]
        pltpu.make_async_copy(src_hbm.at[pl.ds(t, 1)], dst.at[pl.ds(r, 1)], sem).start(priority=priority)


def _wait_rows(n_rows, src_hbm, dst, sem):
    pltpu.make_async_copy(src_hbm.at[pl.ds(0, n_rows)], dst, sem).wait()


def _moe_ffn_kernel(be_ref, nu_ref, nv_ref, tok_ref, h_hbm, wg_ref, wu_ref, wd_ref, y_ref, xbuf, sem):
    i = pl.program_id(0)
    n_used = nu_ref[0]
    n_groups = MOE_ROWS // GATHER_ROWS

    def fetch(block):
        slot = block % 2
        for grp in range(n_groups):
            @pl.when(grp * GATHER_ROWS < nv_ref[block])
            def _():
                _gather_rows(tok_ref, block * MOE_ROWS + grp * GATHER_ROWS, GATHER_ROWS, h_hbm,
                             xbuf.at[slot, pl.ds(grp * GATHER_ROWS, GATHER_ROWS)], sem.at[slot, grp],
                             priority=1)

    @pl.when(i == 0)
    def _():
        xbuf[...] = jnp.zeros_like(xbuf)
        fetch(0)

    @pl.when(i + 1 < n_used)
    def _():
        fetch(i + 1)

    @pl.when(i < n_used)
    def _():
        slot = i % 2
        for grp in range(n_groups):
            @pl.when(grp * GATHER_ROWS < nv_ref[i])
            def _():
                _wait_rows(GATHER_ROWS, h_hbm, xbuf.at[slot, pl.ds(grp * GATHER_ROWS, GATHER_ROWS)],
                           sem.at[slot, grp])
        x = xbuf[slot].astype(BF16)
        gate = _dot(x, wg_ref[0, 0].astype(BF16))
        up = _dot(x, wu_ref[0, 0].astype(BF16))
        act = (gate * jax.nn.sigmoid(gate) * up).astype(BF16)
        y_ref[...] = _dot(act, wd_ref[0, 0].astype(BF16))

    @pl.when(i >= n_used)
    def _():
        y_ref[...] = jnp.zeros_like(y_ref)


def _moe_ffn(h_all, plan, w_gate, w_up, w_down, layer):
    blk_expert, n_used, n_valid, tok_of_slot = plan
    n_blocks = tok_of_slot.shape[0] // MOE_ROWS
    rows = lambda i, be, nu, nv, tok: (i, 0)
    wmap = lambda i, be, nu, nv, tok: (layer, be[i], 0, 0)
    return pl.pallas_call(
        _moe_ffn_kernel,
        grid_spec=pltpu.PrefetchScalarGridSpec(
            num_scalar_prefetch=4,
            grid=(n_blocks,),
            in_specs=[
                pl.BlockSpec(memory_space=pl.ANY),
                pl.BlockSpec((1, 1, D_MODEL, E_FF), wmap),
                pl.BlockSpec((1, 1, D_MODEL, E_FF), wmap),
                pl.BlockSpec((1, 1, E_FF, D_MODEL), wmap),
            ],
            out_specs=pl.BlockSpec((MOE_ROWS, D_MODEL), rows),
            scratch_shapes=[pltpu.VMEM((2, MOE_ROWS, D_MODEL), F32),
                            pltpu.SemaphoreType.DMA((2, MOE_ROWS // GATHER_ROWS))],
        ),
        out_shape=jax.ShapeDtypeStruct((n_blocks * MOE_ROWS, D_MODEL), F32),
        compiler_params=_cparams(("arbitrary",), unchecked_dma=True),
        name="moe_ffn",
    )(blk_expert, n_used, n_valid, tok_of_slot, h_all, w_gate, w_up, w_down)


def _moe_plan_kernel(meta_ref, cnt_ref, be_ref, nu_ref, nv_ref, tok_ref, dp_ref, ds_ref, start_scr,
                     *, n_p, n_s, n_blocks):
    n_tok = n_p + n_s

    def segment(e, first_blk):
        start_scr[e] = first_blk * MOE_ROWS
        cnt = cnt_ref[e]
        nb = (cnt + MOE_ROWS - 1) // MOE_ROWS

        def fill(j, carry):
            be_ref[first_blk + j] = e
            nv_ref[first_blk + j] = jnp.minimum(cnt - j * MOE_ROWS, MOE_ROWS)
            return carry
        lax.fori_loop(0, nb, fill, 0)
        return first_blk + nb
    n_used = lax.fori_loop(0, N_EXPERTS, segment, 0)
    nu_ref[0] = n_used

    last_expert = be_ref[jnp.maximum(n_used - 1, 0)]

    def tail(j, carry):
        be_ref[j] = last_expert
        nv_ref[j] = 0
        return carry
    lax.fori_loop(n_used, n_blocks, tail, 0)

    def clear(j, carry):
        tok_ref[j] = 0
        return carry
    lax.fori_loop(0, n_blocks * MOE_ROWS, clear, 0, unroll=8)

    def place(dest_ref, first, count):
        def body(j, carry):
            t = first + j
            for k in range(2):
                d = start_scr[meta_ref[k * n_tok + t]] + meta_ref[(2 + k) * n_tok + t]
                tok_ref[d] = t
                dest_ref[k * count + j] = d
            return carry
        lax.fori_loop(0, count, body, 0, unroll=4)
    place(dp_ref, 0, n_p)
    place(ds_ref, n_p, n_s)


def _moe_plan(meta, counts, *, n_p, n_s):
    n_blocks = (2 * (n_p + n_s)) // MOE_ROWS + N_EXPERTS
    smem = pl.BlockSpec(memory_space=pltpu.SMEM)
    i32 = lambda n: jax.ShapeDtypeStruct((n,), jnp.int32)
    be, nu, nv, tok, dp, ds = pl.pallas_call(
        functools.partial(_moe_plan_kernel, n_p=n_p, n_s=n_s, n_blocks=n_blocks),
        in_specs=[smem, smem],
        out_specs=[smem] * 6,
        out_shape=[i32(n_blocks), i32(1), i32(n_blocks), i32(n_blocks * MOE_ROWS), i32(2 * n_p), i32(2 * n_s)],
        scratch_shapes=[pltpu.SMEM((N_EXPERTS,), jnp.int32)],
        name="moe_plan",
    )(meta, counts)
    return (be, nu, nv, tok), dp, ds


def _moe_combine_kernel(dest_ref, x_ref, g_ref, y_hbm, o_ref, ybuf, sem, *, tm, n_tiles):
    i = pl.program_id(0)

    def fetch(tile):
        slot = tile % 2
        for k in range(2):
            _gather_rows(dest_ref, (k * n_tiles + tile) * tm, tm, y_hbm, ybuf.at[slot, k], sem.at[slot, k])

    @pl.when(i == 0)
    def _():
        fetch(0)

    @pl.when(i + 1 < n_tiles)
    def _():
        fetch(i + 1)

    slot = i % 2
    acc = x_ref[...]
    for k in range(2):
        _wait_rows(tm, y_hbm, ybuf.at[slot, k], sem.at[slot, k])
        acc = acc + g_ref[:, k:k + 1] * ybuf[slot, k]
    o_ref[...] = acc


def _moe_combine(x, gates, dest, y_rows, *, tm):
    m = x.shape[0]
    n_tiles = m // tm
    return pl.pallas_call(
        functools.partial(_moe_combine_kernel, tm=tm, n_tiles=n_tiles),
        grid_spec=pltpu.PrefetchScalarGridSpec(
            num_scalar_prefetch=1,
            grid=(n_tiles,),
            in_specs=[
                pl.BlockSpec((tm, D_MODEL), lambda i, d: (i, 0)),
                pl.BlockSpec((tm, LANES), lambda i, d: (i, 0)),
                pl.BlockSpec(memory_space=pl.ANY),
            ],
            out_specs=pl.BlockSpec((tm, D_MODEL), lambda i, d: (i, 0)),
            scratch_shapes=[pltpu.VMEM((2, 2, tm, D_MODEL), F32), pltpu.SemaphoreType.DMA((2, 2))],
        ),
        out_shape=jax.ShapeDtypeStruct((m, D_MODEL), F32),
        compiler_params=_cparams(("arbitrary",), unchecked_dma=True),
        name="moe_combine",
    )(dest, x, gates, y_rows)


def _moe(xp, xs, g, w_rg, b_rg, w_re, b_re, w_gate, w_up, w_down, layer):
    n_p, n_s = xp.shape[0], xs.shape[0]
    pad = LANES - E_GROUPS - N_EXPERTS
    w_all = jnp.pad(jnp.concatenate([w_rg, w_re.reshape(D_MODEL, N_EXPERTS)], axis=1), ((0, 0), (0, pad)))
    b_all = jnp.pad(jnp.concatenate([b_rg, b_re.reshape(N_EXPERTS)]), (0, pad)).reshape(1, LANES)
    h_all, mip, mfp, cnt_p = _router(xp, g, w_all, b_all, jnp.zeros((1, LANES), F32), tm=512)
    h_all, mis, mfs, cnt = _router(xs, g, w_all, b_all, cnt_p, tm=n_s, h_into=h_all, h_row0=n_p)
    meta = jnp.concatenate([mip[:4], mis[:4]], axis=1).reshape(-1)
    plan, dest_p, dest_s = _moe_plan(meta, cnt[0].astype(jnp.int32), n_p=n_p, n_s=n_s)
    y_rows = _moe_ffn(h_all, plan, w_gate, w_up, w_down, layer)
    return (_moe_combine(xp, mfp, dest_p, y_rows, tm=COMBINE_ROWS),
            _moe_combine(xs, mfs, dest_s, y_rows, tm=n_s))


def _final_norm_kernel(x_ref, g_ref, o_ref):
    o_ref[...] = _rmsnorm_rows(x_ref[...], g_ref[...])


def _final_norm(x, g, *, tm):
    m = x.shape[0]
    return pl.pallas_call(
        _final_norm_kernel,
        grid=(m // tm,),
        in_specs=[pl.BlockSpec((tm, D_MODEL), lambda i: (i, 0)), pl.BlockSpec((1, D_MODEL), lambda i: (0, 0))],
        out_specs=pl.BlockSpec((tm, D_MODEL), lambda i: (i, 0)),
        out_shape=jax.ShapeDtypeStruct((m, D_MODEL), F32),
        compiler_params=_cparams(("parallel",)),
        name="final_norm",
    )(x, g.reshape(1, D_MODEL))


def _rope_tables(pos):
    half = HEAD_DIM // 2
    inv = 1.0 / (ROPE_THETA ** (jnp.arange(half, dtype=F32) / half))
    ang = pos.astype(F32)[:, None] * inv[None, :]
    cos, sin = jnp.cos(ang), jnp.sin(ang)
    return jnp.concatenate([cos, cos], -1), jnp.concatenate([-sin, sin], -1)


def _kv_state(proj3, group, keep):
    b, t, _ = proj3.shape
    base = 2 * A_WIDTH + group * 3 * B_HW
    k = proj3[:, t - keep:, base + B_HW: base + 2 * B_HW]
    v = proj3[:, t - keep:, base + 2 * B_HW: base + 3 * B_HW]
    return jnp.stack([k, v], axis=2).reshape(b, keep, 2, B_HEADS, HEAD_DIM)


def _layer_ab(xp, xs, g, w_in_b, ln_g, ln_b, w_s, b_s, w_out_b, cache_views, layer, tabs_p, tabs_s, shapes):
    bp, tp, bs, ts = shapes

    proj_p = _inproj(xp, g, w_in_b, layer, tabs_p, tm=1024)
    a_p = _gmlp(proj_p, ln_g, ln_b, w_s, b_s, n_rows=bp * tp, chunks=2, emit_v=False)[0]
    outs, lses = zip(*[_attn_prompt(proj_p, grp, batch=bp, seq=tp) for grp in range(len(B_CONFIGS))])
    b_p = _merge(outs, lses, tm=1024)
    xp_new = _outproj([a_p, b_p], w_out_b, layer, xp, tm=1024, tn=1024)
    proj_p3 = proj_p.reshape(bp, tp, AB_IN)
    kv_p = [_kv_state(proj_p3, grp, min(w, tp)) for grp, (w, _) in enumerate(B_CONFIGS)]

    n_s = bs * ts
    proj_s = _inproj(xs, g, w_in_b, layer, tabs_s, tm=n_s)
    proj_s3 = proj_s.reshape(bs, ts, AB_IN)
    chunk_in = jnp.pad(proj_s3[:, :, :2 * A_WIDTH], ((0, 0), (0, A_CHUNK - ts), (0, 0)))
    a_s, v_s = _gmlp(chunk_in.reshape(bs * A_CHUNK, 2 * A_WIDTH), ln_g, ln_b, w_s, b_s,
                     n_rows=bs * A_CHUNK, chunks=1, emit_v=True)
    a_s = a_s.reshape(bs, A_CHUNK, A_WIDTH)[:, :ts].reshape(n_s, A_WIDTH)
    v_s = v_s.reshape(bs, A_CHUNK, A_WIDTH)[:, :ts]
    qkv_s = jnp.pad(proj_s3[:, :, 2 * A_WIDTH:], ((0, 0), (0, SAMPLE_ROWS - ts), (0, 0)))
    b_s_out = _attn_sample(qkv_s, cache_views, layer)[:, :ts].reshape(n_s, B_HW)
    xs_new = _outproj([a_s, b_s_out], w_out_b, layer, xs, tm=n_s, tn=1024)
    kv_s = [_kv_state(proj_s3, grp, ts) for grp in range(len(B_CONFIGS))]
    return xp_new, xs_new, kv_p, kv_s, v_s


def _layer_c(xp, xs, g, w_main, w_dt, conv_w, conv_b, dt_bias, a_log, d_skip, norm_g, w_out_b, layer,
             conv_state, ssm_state, shapes):
    bp, tp, bs, ts = shapes
    n_main = C_D_INNER + C_CONV_DIM
    args = (conv_w, conv_b, dt_bias, a_log, d_skip, norm_g)

    proj_p = _inproj(xp, g, w_main, layer, tm=1024, n_out=n_main)
    dt_p = _inproj(xp, g, w_dt, layer, tm=1024)
    zc = jnp.zeros((bp, CONV_PAD, C_CONV_DIM), F32)
    zh = jnp.zeros((bp, C_GROUPS, C_D_STATE, C_GW), F32)
    q_p = 128
    y_p, h_p = _ssd(proj_p, dt_p, zc, zh, *args, batch=bp, n_chunks=tp // q_p, q_len=q_p, t_valid=q_p)
    xp_new = _outproj([y_p], w_out_b, layer, xp, tm=512, tn=1024)
    conv_p = proj_p.reshape(bp, tp, n_main)[:, tp - (C_CONV - 1):, C_D_INNER:]

    n_s = bs * ts
    q_s = 16
    proj_s = _inproj(xs, g, w_main, layer, tm=n_s, n_out=n_main)
    dt_s = _inproj(xs, g, w_dt, layer, tm=n_s)
    pad_rows = lambda a: jnp.pad(a.reshape(bs, ts, -1), ((0, 0), (0, q_s - ts), (0, 0))).reshape(bs * q_s, -1)
    cst = jnp.pad(conv_state, ((0, 0), (CONV_PAD - (C_CONV - 1), 0), (0, 0)))
    y_s, h_s = _ssd(pad_rows(proj_s), pad_rows(dt_s), cst, _state_to_t(ssm_state), *args,
                    batch=bs, n_chunks=1, q_len=q_s, t_valid=ts)
    y_s = y_s.reshape(bs, q_s, C_D_INNER)[:, :ts].reshape(n_s, C_D_INNER)
    xs_new = _outproj([y_s], w_out_b, layer, xs, tm=n_s, tn=1024)
    assert ts >= C_CONV - 1
    conv_s = proj_s.reshape(bs, ts, n_main)[:, ts - (C_CONV - 1):, C_D_INNER:]
    return xp_new, xs_new, conv_p, conv_s, _state_from_t(h_p), _state_from_t(h_s)


def kernel(x_prompt, x_sample, cache_kv_w128, cache_kv_w512, cache_kv_w2048, state_conv, state_ssm,
           norm_mix, norm_ffn, norm_final, w_in_ab, a_ln_g, a_ln_b, a_w_s, a_b_s, w_out_ab,
           w_in_c, c_conv_w, c_conv_b, c_dt_bias, c_a_log, c_d, c_norm_g, w_out_c,
           w_router_g, b_router_g, w_router_e, b_router_e, w_exp_gate, w_exp_up, w_exp_down):
    bp, tp, _ = x_prompt.shape
    bs, ts, _ = x_sample.shape
    shapes = (bp, tp, bs, ts)
    depth = norm_mix.shape[0]
    kv_caches = (cache_kv_w128, cache_kv_w512, cache_kv_w2048)
    tabs_p = _rope_tables(jnp.arange(tp, dtype=jnp.int32))
    pos_s = PAST_LEN + jnp.arange(ts, dtype=jnp.int32)
    tabs_s = tuple(jnp.tile(t, (bs, 1)) for t in _rope_tables(pos_s))

    w_in_ab_b = w_in_ab.astype(BF16)
    w_out_ab_b = w_out_ab.astype(BF16)
    w_in_c_b = w_in_c.astype(BF16)
    w_dt_b = jnp.pad(w_in_c[:, :, C_D_INNER + C_CONV_DIM:], ((0, 0), (0, 0), (0, LANES - C_HEADS))).astype(BF16)
    w_out_c_b = w_out_c.astype(BF16)
    cache_views = _cache_views(kv_caches, ts)

    xp = x_prompt.reshape(bp * tp, D_MODEL)
    xs = x_sample.reshape(bs * ts, D_MODEL)
    kv_p = [[] for _ in B_CONFIGS]
    kv_s = [[] for _ in B_CONFIGS]
    chunk_v, conv_p, conv_s, ssm_p, ssm_s = [], [], [], [], []
    for l in range(depth):
        i = l // 2
        if l % 2 == 0:
            xp, xs, nkv_p, nkv_s, v_s = _layer_ab(
                xp, xs, norm_mix[l], w_in_ab_b, a_ln_g[i], a_ln_b[i], a_w_s[i], a_b_s[i], w_out_ab_b,
                cache_views, i, tabs_p, tabs_s, shapes)
            for grp in range(len(B_CONFIGS)):
                kv_p[grp].append(nkv_p[grp])
                kv_s[grp].append(nkv_s[grp])
            chunk_v.append(v_s)
        else:
            xp, xs, ncp, ncs, nsp, nss = _layer_c(
                xp, xs, norm_mix[l], w_in_c_b, w_dt_b, c_conv_w[i], c_conv_b[i], c_dt_bias[i], c_a_log[i], c_d[i],
                c_norm_g[i], w_out_c_b, i, state_conv[i], state_ssm[i], shapes)
            conv_p.append(ncp)
            conv_s.append(ncs)
            ssm_p.append(nsp)
            ssm_s.append(nss)
        xp, xs = _moe(xp, xs, norm_ffn[l], w_router_g[l], b_router_g[l], w_router_e[l], b_router_e[l],
                      w_exp_gate, w_exp_up, w_exp_down, l)
    y_p = _final_norm(xp, norm_final, tm=1024).reshape(bp, tp, D_MODEL)
    y_s = _final_norm(xs, norm_final, tm=bs * ts).reshape(bs, ts, D_MODEL)
    kv_s = [jnp.concatenate([c[:, :, ts:], jnp.stack(new)], axis=2) for c, new in zip(kv_caches, kv_s)]
    return (y_p, y_s,
            jnp.stack(kv_p[0]), kv_s[0],
            jnp.stack(kv_p[1]), kv_s[1],
            jnp.stack(kv_p[2]), kv_s[2],
            jnp.stack(chunk_v),
            jnp.stack(conv_p), jnp.stack(conv_s),
            jnp.stack(ssm_p), jnp.stack(ssm_s))
```

```python
import functools
import math

import jax
import jax.numpy as jnp
from jax import lax
from jax.experimental import pallas as pl
from jax.experimental.pallas import tpu as pltpu

F32 = jnp.float32
BF16 = jnp.bfloat16

D_MODEL = 2048
PAST_LEN = 16384
EPS = 1e-6
A_WIDTH = 1024
A_GROUPS = 8
A_GW = 128
A_CHUNK = 128
HEAD_DIM = 128
B_HEADS = 4
B_CONFIGS = ((128, 1), (512, 4), (2048, 16))
B_BAND = 128
B_HW = B_HEADS * HEAD_DIM
AB_IN = 2 * A_WIDTH + 3 * len(B_CONFIGS) * B_HW
ROPE_THETA = 10000.0
C_D_INNER = 4096
C_HEAD_DIM = 64
C_HEADS = 64
C_GROUPS = 8
C_D_STATE = 128
C_CONV = 4
C_CONV_DIM = C_D_INNER + 2 * C_GROUPS * C_D_STATE
C_GW = C_D_INNER // C_GROUPS
E_GROUPS = 8
E_PER_GROUP = 8
N_EXPERTS = 64
E_FF = 512
MOE_ROWS = 256
GATHER_ROWS = 32
COMBINE_ROWS = 128
META_ROWS = 8

LANES = 128
VMEM_LIMIT = 56 * 1024 * 1024


def _cparams(sem, unchecked_dma=False):
    return pltpu.CompilerParams(dimension_semantics=sem, vmem_limit_bytes=VMEM_LIMIT,
                                disable_bounds_checks=unchecked_dma)


def _split3(x):
    hi = x.astype(BF16)
    r1 = x - hi.astype(F32)
    mid = r1.astype(BF16)
    lo = (r1 - mid.astype(F32)).astype(BF16)
    return hi, mid, lo


def _dot(a, b):
    return jnp.dot(a, b, preferred_element_type=F32)


def _dot_nt(a, b):
    return lax.dot_general(a, b, (((1,), (1,)), ((), ())), preferred_element_type=F32)


def _rmsnorm_rows(x, g):
    ms = jnp.mean(x * x, axis=-1, keepdims=True)
    return x * lax.rsqrt(ms + EPS) * g


def _inproj_kernel(x_ref, g_ref, w_ref, *rest, rope):
    o_ref, h_scr = rest[-2:]
    j = pl.program_id(1)

    @pl.when(j == 0)
    def _():
        h_scr[...] = _rmsnorm_rows(x_ref[...], g_ref[...]).astype(BF16)

    acc = _dot(h_scr[...], w_ref[0])
    if not rope:
        o_ref[...] = acc
        return
    cos_ref, sin_ref = rest[:2]
    is_qk = jnp.logical_and(j >= 4, (j - 4) % 3 != 2)

    @pl.when(is_qk)
    def _():
        cos = cos_ref[...]
        sin = sin_ref[...]
        for h in range(B_HEADS):
            sl = slice(h * HEAD_DIM, (h + 1) * HEAD_DIM)
            a = acc[:, sl]
            o_ref[:, sl] = a * cos + pltpu.roll(a, HEAD_DIM // 2, 1) * sin

    @pl.when(jnp.logical_not(is_qk))
    def _():
        o_ref[...] = acc


def _inproj(x, g, w_bf16, layer, tabs=None, *, tm, n_out=None):
    m, _ = x.shape
    n = n_out or w_bf16.shape[2]
    tn = B_HW if n % B_HW == 0 else n
    in_specs = [
        pl.BlockSpec((tm, D_MODEL), lambda i, j: (i, 0)),
        pl.BlockSpec((1, D_MODEL), lambda i, j: (0, 0)),
        pl.BlockSpec((1, D_MODEL, tn), lambda i, j: (layer, 0, j)),
    ]
    args = [x, g.reshape(1, D_MODEL), w_bf16]
    if tabs is not None:
        n_pos_tiles = tabs[0].shape[0] // tm
        in_specs += [pl.BlockSpec((tm, HEAD_DIM), lambda i, j: (i % n_pos_tiles, 0))] * 2
        args += list(tabs)
    return pl.pallas_call(
        functools.partial(_inproj_kernel, rope=tabs is not None),
        grid=(m // tm, n // tn),
        in_specs=in_specs,
        out_specs=pl.BlockSpec((tm, tn), lambda i, j: (i, j)),
        out_shape=jax.ShapeDtypeStruct((m, n), F32),
        scratch_shapes=[pltpu.VMEM((tm, D_MODEL), BF16)],
        compiler_params=_cparams(("parallel", "arbitrary")),
        name="inproj",
    )(*args)


def _gmlp_kernel(p_ref, lng_ref, lnb_ref, ws_ref, bs_ref, a_ref, *v_ref, chunks):
    row = lax.broadcasted_iota(jnp.int32, (A_CHUNK, A_CHUNK), 0)
    col = lax.broadcasted_iota(jnp.int32, (A_CHUNK, A_CHUNK), 1)
    causal = col <= row
    for c in range(chunks):
        rows = slice(c * A_CHUNK, (c + 1) * A_CHUNK)
        u = jax.nn.gelu(p_ref[rows, 0:A_WIDTH], approximate=True)
        vg = jax.nn.gelu(p_ref[rows, A_WIDTH:2 * A_WIDTH], approximate=True)
        vc = vg - jnp.mean(vg, axis=-1, keepdims=True)
        v = vc * lax.rsqrt(jnp.mean(vc * vc, axis=-1, keepdims=True) + EPS)
        v = v * lng_ref[...] + lnb_ref[...]
        if v_ref:
            v_ref[0][rows, :] = v
        vb = v.astype(BF16)
        for g in range(A_GROUPS):
            sl = slice(g * A_GW, (g + 1) * A_GW)
            w = jnp.where(causal, ws_ref[g], 0.0).astype(BF16)
            mixed = _dot(w, vb[:, sl]) + bs_ref[:, sl]
            a_ref[rows, sl] = (u[:, sl] * mixed).astype(BF16)


def _gmlp(proj, ln_g, ln_b, w_s, b_s, *, n_rows, chunks, emit_v):
    rows = chunks * A_CHUNK
    bsb = jnp.repeat(jnp.transpose(b_s), A_GW, axis=1)
    out_shape = [jax.ShapeDtypeStruct((n_rows, A_WIDTH), BF16)]
    out_specs = [pl.BlockSpec((rows, A_WIDTH), lambda i: (i, 0))]
    if emit_v:
        out_shape.append(jax.ShapeDtypeStruct((n_rows, A_WIDTH), F32))
        out_specs.append(pl.BlockSpec((rows, A_WIDTH), lambda i: (i, 0)))
    return pl.pallas_call(
        functools.partial(_gmlp_kernel, chunks=chunks),
        grid=(n_rows // rows,),
        in_specs=[
            pl.BlockSpec((rows, 2 * A_WIDTH), lambda i: (i, 0)),
            pl.BlockSpec((1, A_WIDTH), lambda i: (0, 0)),
            pl.BlockSpec((1, A_WIDTH), lambda i: (0, 0)),
            pl.BlockSpec((A_GROUPS, A_CHUNK, A_CHUNK), lambda i: (0, 0, 0)),
            pl.BlockSpec((A_CHUNK, A_WIDTH), lambda i: (0, 0)),
        ],
        out_specs=out_specs,
        out_shape=out_shape,
        compiler_params=_cparams(("parallel",)),
        name="gmlp",
    )(proj, ln_g.reshape(1, A_WIDTH), ln_b.reshape(1, A_WIDTH), w_s, bsb)


def _attn_prompt_kernel(q_ref, kp_ref, kc_ref, vp_ref, vc_ref, o_ref, lse_ref, *, dil, heads):
    n = pl.program_id(2)
    qi = lax.broadcasted_iota(jnp.int32, (B_BAND, 2 * B_BAND), 0)
    kc_idx = lax.broadcasted_iota(jnp.int32, (B_BAND, 2 * B_BAND), 1)
    dist = qi + B_BAND - kc_idx
    ok = jnp.logical_and(jnp.logical_and(dist >= 0, dist <= B_BAND),
                         jnp.logical_or(kc_idx >= B_BAND, n > 0))
    scale = HEAD_DIM ** -0.5
    for r in range(dil):
        rows = pl.ds(r, B_BAND, stride=dil) if dil > 1 else slice(None)
        for h in range(heads):
            sl = slice(h * HEAD_DIM, (h + 1) * HEAD_DIM)
            q = q_ref[rows, sl].astype(BF16)
            k = jnp.concatenate([kp_ref[rows, sl], kc_ref[rows, sl]], axis=0).astype(BF16)
            v = jnp.concatenate([vp_ref[rows, sl], vc_ref[rows, sl]], axis=0).astype(BF16)
            s = jnp.where(ok, _dot_nt(q, k) * scale, -jnp.inf)
            m = jnp.max(s, axis=-1, keepdims=True)
            p = jnp.exp(s - m)
            l = jnp.sum(p, axis=-1, keepdims=True)
            o_ref[rows, sl] = _dot(p.astype(BF16), v) / l
            lse_ref[rows, sl] = jnp.broadcast_to(m + jnp.log(l), (B_BAND, HEAD_DIM))


def _attn_prompt(proj, group, *, batch, seq):
    dil = B_CONFIGS[group][1]
    rows = B_BAND * dil
    nblk = seq // rows
    heads = B_HEADS if dil == 1 else 1
    width = heads * HEAD_DIM
    n_col = B_HW // width
    qcol = (4 + 3 * group) * n_col

    def spec(off, prev):
        def imap(b, h, n):
            nn = jnp.maximum(n - 1, 0) if prev else n
            return (b * nblk + nn, qcol + off * n_col + h)
        return pl.BlockSpec((rows, width), imap)

    out_spec = pl.BlockSpec((rows, width), lambda b, h, n: (b * nblk + n, h))
    return pl.pallas_call(
        functools.partial(_attn_prompt_kernel, dil=dil, heads=heads),
        grid=(batch, n_col, nblk),
        in_specs=[spec(0, False), spec(1, True), spec(1, False), spec(2, True), spec(2, False)],
        out_specs=[out_spec, out_spec],
        out_shape=[jax.ShapeDtypeStruct((batch * seq, B_HW), F32)] * 2,
        compiler_params=_cparams(("parallel", "parallel", "arbitrary")),
        name=f"attn_prompt_g{group}",
    )(proj, proj, proj, proj, proj)


def _merge_kernel(o0, o1, o2, l0, l1, l2, b_ref):
    m = jnp.maximum(jnp.maximum(l0[...], l1[...]), l2[...])
    e0 = jnp.exp(l0[...] - m)
    e1 = jnp.exp(l1[...] - m)
    e2 = jnp.exp(l2[...] - m)
    den = e0 + e1 + e2
    b_ref[...] = ((e0 / den) * o0[...] + (e1 / den) * o1[...] + (e2 / den) * o2[...]).astype(BF16)


def _merge(outs, lses, *, tm):
    m = outs[0].shape[0]
    spec = pl.BlockSpec((tm, B_HW), lambda i: (i, 0))
    return pl.pallas_call(
        _merge_kernel,
        grid=(m // tm,),
        in_specs=[spec] * 6,
        out_specs=spec,
        out_shape=jax.ShapeDtypeStruct((m, B_HW), BF16),
        compiler_params=_cparams(("parallel",)),
        name="attn_merge",
    )(*outs, *lses)


SAMPLE_ROWS = 16


def _attn_sample_kernel(qkv_ref, c0_ref, c1_ref, c2_ref, b_ref, *, n_real):
    caches = (c0_ref, c1_ref, c2_ref)
    row_w = 2 * B_HW
    scale = HEAD_DIM ** -0.5
    qi_c = lax.broadcasted_iota(jnp.int32, (SAMPLE_ROWS, B_BAND), 0)
    ka_c = lax.broadcasted_iota(jnp.int32, (SAMPLE_ROWS, B_BAND), 1)
    qi_n = lax.broadcasted_iota(jnp.int32, (SAMPLE_ROWS, SAMPLE_ROWS), 0)
    km_n = lax.broadcasted_iota(jnp.int32, (SAMPLE_ROWS, SAMPLE_ROWS), 1)

    def valid(diff, dil):
        shift = dil.bit_length() - 1
        ok = jnp.logical_and(diff >= 0, (diff & (dil - 1)) == 0)
        return jnp.logical_and(ok, (diff >> shift) <= B_BAND)

    for h in range(B_HEADS):
        hs = slice(h * HEAD_DIM, (h + 1) * HEAD_DIM)
        outs, lses = [], []
        for g, (window, dil) in enumerate(B_CONFIGS):
            base = g * 3 * B_HW
            q = qkv_ref[0, :, base + h * HEAD_DIM: base + (h + 1) * HEAD_DIM].astype(BF16)
            kn = qkv_ref[0, :, base + B_HW + h * HEAD_DIM: base + B_HW + (h + 1) * HEAD_DIM].astype(BF16)
            vn = qkv_ref[0, :, base + 2 * B_HW + h * HEAD_DIM: base + 2 * B_HW + (h + 1) * HEAD_DIM].astype(BF16)
            n_res = min(dil, n_real)
            scores, values = [], []
            for r in range(n_res):
                kc = caches[g][0, 0, :, r * row_w + h * HEAD_DIM: r * row_w + (h + 1) * HEAD_DIM].astype(BF16)
                vc = caches[g][0, 0, :, r * row_w + B_HW + h * HEAD_DIM: r * row_w + B_HW + (h + 1) * HEAD_DIM].astype(BF16)
                s = _dot_nt(q, kc) * scale
                diff = qi_c + window - ka_c * dil - r
                scores.append(jnp.where(valid(diff, dil), s, -jnp.inf))
                values.append(vc)
            s = _dot_nt(q, kn) * scale
            scores.append(jnp.where(valid(qi_n - km_n, dil), s, -jnp.inf))
            values.append(vn)
            m = functools.reduce(jnp.maximum, [jnp.max(s, axis=-1, keepdims=True) for s in scores])
            ps = [jnp.exp(s - m) for s in scores]
            l = functools.reduce(jnp.add, [jnp.sum(p, axis=-1, keepdims=True) for p in ps])
            o = functools.reduce(jnp.add, [_dot(p.astype(BF16), v) for p, v in zip(ps, values)])
            outs.append(o / l)
            lses.append(m + jnp.log(l))
        m = functools.reduce(jnp.maximum, lses)
        es = [jnp.exp(l - m) for l in lses]
        den = functools.reduce(jnp.add, es)
        b_ref[0, :, hs] = functools.reduce(jnp.add, [(e / den) * o for e, o in zip(es, outs)]).astype(BF16)


def _cache_views(caches, n_real):
    views = []
    for (window, dil), c in zip(B_CONFIGS, caches):
        layers, b = c.shape[:2]
        assert c.shape[2] == window and window == dil * B_BAND
        n_res = min(dil, n_real)
        c = c.reshape(layers, b, B_BAND, dil, 2, B_HEADS, HEAD_DIM)[:, :, :, :n_res]
        views.append(c.reshape(layers, b, B_BAND, n_res * 2 * B_HW))
    return views


def _attn_sample(qkv, views, layer):
    b = qkv.shape[0]
    n_real = max(v.shape[3] // (2 * B_HW) for v in views)
    specs = [pl.BlockSpec((1, 1) + v.shape[2:], lambda i: (layer, i, 0, 0)) for v in views]
    return pl.pallas_call(
        functools.partial(_attn_sample_kernel, n_real=n_real),
        grid=(b,),
        in_specs=[pl.BlockSpec((1, SAMPLE_ROWS, qkv.shape[2]), lambda i: (i, 0, 0))] + specs,
        out_specs=pl.BlockSpec((1, SAMPLE_ROWS, B_HW), lambda i: (i, 0, 0)),
        out_shape=jax.ShapeDtypeStruct((b, SAMPLE_ROWS, B_HW), BF16),
        compiler_params=_cparams(("parallel",)),
        name="attn_sample",
    )(qkv, *views)


def _outproj_kernel(*refs, n_parts):
    a_refs = refs[:n_parts]
    w_refs = refs[n_parts:2 * n_parts]
    x_ref, o_ref = refs[2 * n_parts], refs[2 * n_parts + 1]
    acc = x_ref[...]
    for a, w in zip(a_refs, w_refs):
        acc = acc + _dot(a[...], w[0])
    o_ref[...] = acc


def _outproj(parts, w_bf16, layer, x, *, tm, tn):
    m, n = x.shape
    in_specs = [pl.BlockSpec((tm, a.shape[1]), lambda j, i: (i, 0)) for a in parts]
    row0 = 0
    for a in parts:
        width = a.shape[1]
        blk = row0 // width
        assert blk * width == row0
        in_specs.append(pl.BlockSpec((1, width, tn), lambda j, i, blk=blk: (layer, blk, j)))
        row0 += width
    assert row0 == w_bf16.shape[1]
    weights = [w_bf16] * len(parts)
    in_specs.append(pl.BlockSpec((tm, tn), lambda j, i: (i, j)))
    return pl.pallas_call(
        functools.partial(_outproj_kernel, n_parts=len(parts)),
        grid=(n // tn, m // tm),
        in_specs=in_specs,
        out_specs=pl.BlockSpec((tm, tn), lambda j, i: (i, j)),
        out_shape=jax.ShapeDtypeStruct((m, n), F32),
        compiler_params=_cparams(("parallel", "parallel")),
        name="outproj",
    )(*parts, *weights, x)


CONV_PAD = 8


def _ssd_kernel(z_ref, xa_ref, xb_ref, bc_ref, dtr_ref, dtrt_ref, cst_ref, cw_ref, cb_ref,
                dtb_ref, dtbc_ref, alog_ref, alogc_ref, dsk_ref, ng_ref, e_ref, h0_ref,
                y_ref, h_ref, ext_scr, act_scr, *, q_len, t_valid):
    c = pl.program_id(1)

    @pl.when(c == 0)
    def _():
        ext_scr[0:CONV_PAD, :] = cst_ref[0]
        h_ref[...] = h0_ref[...]

    half = C_CONV_DIM // 3
    ext_scr[CONV_PAD:CONV_PAD + q_len, 0:half] = xa_ref[...]
    ext_scr[CONV_PAD:CONV_PAD + q_len, half:2 * half] = xb_ref[...]
    ext_scr[CONV_PAD:CONV_PAD + q_len, 2 * half:3 * half] = bc_ref[...]
    for s in range(C_CONV_DIM // C_GW):
        sl = slice(s * C_GW, (s + 1) * C_GW)
        acc = cb_ref[:, sl]
        for j in range(C_CONV):
            lo = CONV_PAD - (C_CONV - 1) + j
            acc = acc + ext_scr[lo:lo + q_len, sl] * cw_ref[j:j + 1, sl]
        act_scr[:, sl] = acc * jax.nn.sigmoid(acc)
    ext_scr[0:CONV_PAD, :] = ext_scr[q_len:q_len + CONV_PAD, :]

    ti = lax.broadcasted_iota(jnp.int32, (q_len, q_len), 0)
    si = lax.broadcasted_iota(jnp.int32, (q_len, q_len), 1)
    tri = si <= ti
    tri_b = jnp.where(tri, 1.0, 0.0).astype(BF16)
    tri_tb = jnp.where(ti <= si, 1.0, 0.0).astype(BF16)

    dt = jax.nn.softplus(dtr_ref[...] + dtb_ref[...])
    dtt = jax.nn.softplus(dtrt_ref[...] + dtbc_ref[...])
    if t_valid < q_len:
        dt = jnp.where(lax.broadcasted_iota(jnp.int32, dt.shape, 0) < t_valid, dt, 0.0)
        dtt = jnp.where(lax.broadcasted_iota(jnp.int32, dtt.shape, 1) < t_valid, dtt, 0.0)
    da = dt * (-jnp.exp(alog_ref[...]))
    dat = dtt * (-jnp.exp(alogc_ref[...]))
    a_cum = functools.reduce(jnp.add, [_dot(tri_b, p) for p in reversed(_split3(da))])
    a_cumt = functools.reduce(jnp.add, [_dot(p, tri_tb) for p in reversed(_split3(dat))])
    dt_parts = _split3(dt)
    ac_parts = _split3(a_cum)

    lane_lo = lax.broadcasted_iota(jnp.int32, (q_len, LANES), 1) < C_HEAD_DIM

    for g in range(C_GROUPS):
        gs = slice(g * C_GW, (g + 1) * C_GW)
        e_g = e_ref[:, gs]
        dt_x = functools.reduce(jnp.add, [_dot(p, e_g) for p in reversed(dt_parts)])
        ac_x = functools.reduce(jnp.add, [_dot(p, e_g) for p in reversed(ac_parts)])
        xs = act_scr[:, gs]
        bm = act_scr[:, C_D_INNER + g * C_D_STATE: C_D_INNER + (g + 1) * C_D_STATE]
        cm = act_scr[:, C_D_INNER + C_GROUPS * C_D_STATE + g * C_D_STATE:
                     C_D_INNER + C_GROUPS * C_D_STATE + (g + 1) * C_D_STATE]
        bm_b = bm.astype(BF16)
        cm_b = cm.astype(BF16)
        xdt = xs * dt_x
        a_last = ac_x[q_len - 1:q_len, :]
        xdt_b = xdt.astype(BF16)
        xdt_end_b = (xdt * jnp.exp(a_last - ac_x)).astype(BF16)
        cbm = _dot_nt(cm_b, bm_b)

        h_prev = h_ref[0, g]
        y = _dot(cm_b, h_prev.astype(BF16)) * jnp.exp(ac_x)
        st = _dot(jnp.transpose(bm).astype(BF16), xdt_end_b)
        h_ref[0, g] = h_prev * jnp.exp(a_last) + st

        pairs = []
        for hp in range(C_GW // LANES):
            ms = []
            for hh in (2 * hp, 2 * hp + 1):
                hd = g * (C_GW // C_HEAD_DIM) + hh
                seg = a_cum[:, hd:hd + 1] - a_cumt[hd:hd + 1, :]
                decay = jnp.exp(jnp.where(tri, seg, -jnp.inf))
                ms.append((cbm * decay).astype(BF16))
            slab = xdt_b[:, hp * LANES:(hp + 1) * LANES]
            zero = jnp.zeros_like(slab)
            rhs = jnp.concatenate([jnp.where(lane_lo, slab, zero), jnp.where(lane_lo, zero, slab)], axis=0)
            pairs.append(_dot(jnp.concatenate(ms, axis=1), rhs))
        y = y + jnp.concatenate(pairs, axis=1)
        y = y + xs * dsk_ref[:, gs]
        zz = z_ref[:, gs]
        y = y * (zz * jax.nn.sigmoid(zz))
        y = y * lax.rsqrt(jnp.mean(y * y, axis=-1, keepdims=True) + EPS) * ng_ref[:, gs]
        y_ref[:, gs] = y.astype(BF16)


def _ssd(proj, dt_raw, conv_state, h0t, conv_w, conv_b, dt_bias, a_log, d_skip, norm_g,
         *, batch, n_chunks, q_len, t_valid):
    rows = batch * n_chunks * q_len
    dtrt = dt_raw.reshape(batch * n_chunks, q_len, LANES).transpose(0, 2, 1).reshape(-1, q_len)
    pad = LANES - C_HEADS
    col = lambda v: jnp.pad(v, (0, pad)).reshape(LANES, 1)
    row = lambda v: jnp.pad(v, (0, pad)).reshape(1, LANES)
    expand = jnp.repeat(jnp.eye(LANES, C_HEADS, dtype=BF16), C_HEAD_DIM, axis=1)
    cw = jnp.pad(conv_w, ((0, CONV_PAD - C_CONV), (0, 0)))
    xcol = C_D_INNER // (C_CONV_DIM // 3)
    const = lambda shape: pl.BlockSpec(shape, lambda b, c: (0,) * len(shape))
    blk = lambda width, j: pl.BlockSpec((q_len, width), lambda b, c: (b * n_chunks + c, j))
    y, h = pl.pallas_call(
        functools.partial(_ssd_kernel, q_len=q_len, t_valid=t_valid),
        grid=(batch, n_chunks),
        in_specs=[
            blk(C_D_INNER, 0), blk(C_CONV_DIM // 3, xcol), blk(C_CONV_DIM // 3, xcol + 1),
            blk(C_CONV_DIM // 3, xcol + 2), blk(LANES, 0),
            pl.BlockSpec((LANES, q_len), lambda b, c: (b * n_chunks + c, 0)),
            pl.BlockSpec((1, CONV_PAD, C_CONV_DIM), lambda b, c: (b, 0, 0)),
            const((CONV_PAD, C_CONV_DIM)), const((1, C_CONV_DIM)),
            const((1, LANES)), const((LANES, 1)), const((1, LANES)), const((LANES, 1)),
            const((1, C_D_INNER)), const((1, C_D_INNER)), const((LANES, C_D_INNER)),
            pl.BlockSpec((1, C_GROUPS, C_D_STATE, C_GW), lambda b, c: (b, 0, 0, 0)),
        ],
        out_specs=[
            pl.BlockSpec((q_len, C_D_INNER), lambda b, c: (b * n_chunks + c, 0)),
            pl.BlockSpec((1, C_GROUPS, C_D_STATE, C_GW), lambda b, c: (b, 0, 0, 0)),
        ],
        out_shape=[
            jax.ShapeDtypeStruct((rows, C_D_INNER), BF16),
            jax.ShapeDtypeStruct((batch, C_GROUPS, C_D_STATE, C_GW), F32),
        ],
        scratch_shapes=[
            pltpu.VMEM((q_len + CONV_PAD, C_CONV_DIM), F32),
            pltpu.VMEM((q_len, C_CONV_DIM), F32),
        ],
        compiler_params=_cparams(("parallel", "arbitrary")),
        name="ssd",
    )(proj, proj, proj, proj, dt_raw, dtrt, conv_state, cw, conv_b.reshape(1, -1),
      row(dt_bias), col(dt_bias), row(a_log), col(a_log),
      jnp.repeat(d_skip, C_HEAD_DIM).reshape(1, -1), norm_g.reshape(1, -1), expand, h0t)
    return y, h


def _state_to_t(h):
    b = h.shape[0]
    return h.reshape(b, C_GROUPS, C_HEADS // C_GROUPS, C_HEAD_DIM, C_D_STATE).transpose(0, 1, 4, 2, 3).reshape(
        b, C_GROUPS, C_D_STATE, C_GW)


def _state_from_t(ht):
    b = ht.shape[0]
    return ht.reshape(b, C_GROUPS, C_D_STATE, C_HEADS // C_GROUPS, C_HEAD_DIM).transpose(0, 1, 3, 4, 2).reshape(
        b, C_HEADS, C_HEAD_DIM, C_D_STATE)


def _router_kernel(x_ref, g_ref, w_ref, b_ref, base_ref, *rest, n_tiles):
    h_ref, mi_ref, mf_ref, cnt_ref = rest[-4:]
    i = pl.program_id(0)

    @pl.when(i == 0)
    def _():
        cnt_ref[...] = base_ref[...]

    @pl.when(i < n_tiles)
    def _():
        _route_tile(x_ref, g_ref, w_ref, b_ref, h_ref, mi_ref, mf_ref, cnt_ref)

    @pl.when(i >= n_tiles)
    def _():
        h_ref[...] = jnp.zeros_like(h_ref)


def _route_tile(x_ref, g_ref, w_ref, b_ref, h_ref, mi_ref, mf_ref, cnt_ref):
    tm = x_ref.shape[0]
    h = _rmsnorm_rows(x_ref[...], g_ref[...])
    h_ref[...] = h
    h_hi, h_lo, _ = _split3(h)
    w_hi, w_lo, _ = _split3(w_ref[...])
    logits = (_dot(h_lo, w_hi) + _dot(h_hi, w_lo)) + _dot(h_hi, w_hi) + b_ref[...]

    lane = lax.broadcasted_iota(jnp.int32, (tm, LANES), 1).astype(F32)
    big = float(4 * LANES)
    lg = jnp.where(lane < E_GROUPS, logits, -jnp.inf)
    mg = jnp.max(lg, axis=-1, keepdims=True)
    p_top = 1.0 / jnp.sum(jnp.exp(lg - mg), axis=-1, keepdims=True)
    g_top = jnp.min(jnp.where(lg == mg, lane, big), axis=-1, keepdims=True)
    lo = E_GROUPS + g_top * E_PER_GROUP
    le = jnp.where(jnp.logical_and(lane >= lo, lane < lo + E_PER_GROUP), logits, -jnp.inf)
    v1 = jnp.max(le, axis=-1, keepdims=True)
    i1 = jnp.min(jnp.where(le == v1, lane, big), axis=-1, keepdims=True)
    le2 = jnp.where(lane == i1, -jnp.inf, le)
    v2 = jnp.max(le2, axis=-1, keepdims=True)
    i2 = jnp.min(jnp.where(le2 == v2, lane, big), axis=-1, keepdims=True)
    e21 = jnp.exp(v2 - v1)
    gate1 = p_top / (1.0 + e21)
    gate2 = p_top * e21 / (1.0 + e21)
    eid1 = i1 - E_GROUPS
    eid2 = i2 - E_GROUPS

    oh1 = lane == eid1
    oh2 = lane == eid2
    oh = jnp.where(jnp.logical_or(oh1, oh2), 1.0, 0.0)
    ri = lax.broadcasted_iota(jnp.int32, (tm, tm), 0)
    ci = lax.broadcasted_iota(jnp.int32, (tm, tm), 1)
    before = jnp.where(ci < ri, 1.0, 0.0).astype(BF16)
    pos = _dot(before, oh.astype(BF16)) + cnt_ref[...]
    rank1 = jnp.sum(jnp.where(oh1, pos, 0.0), axis=-1, keepdims=True)
    rank2 = jnp.sum(jnp.where(oh2, pos, 0.0), axis=-1, keepdims=True)
    cnt_ref[...] = cnt_ref[...] + jnp.sum(oh, axis=0, keepdims=True)

    mi = jnp.where(lane == 0, eid1, jnp.where(lane == 1, eid2, jnp.where(lane == 2, rank1,
                                                                         jnp.where(lane == 3, rank2, 0.0))))
    sel = jnp.where(lax.broadcasted_iota(jnp.int32, (META_ROWS, LANES), 0)
                    == lax.broadcasted_iota(jnp.int32, (META_ROWS, LANES), 1), 1.0, 0.0).astype(BF16)
    mi_t = functools.reduce(jnp.add, [_dot_nt(sel, p) for p in _split3(mi)])
    mi_ref[...] = mi_t.astype(jnp.int32)
    mf_ref[...] = jnp.where(lane == 0, gate1, jnp.where(lane == 1, gate2, 0.0))


def _router(x, g, w_all, b_all, base, *, tm, h_into=None, h_row0=0):
    m = x.shape[0]
    n_tiles = m // tm
    last = n_tiles - 1
    tile = lambda i: jnp.minimum(i, last)
    one = pl.BlockSpec((1, LANES), lambda i: (0, 0))
    in_specs = [
        pl.BlockSpec((tm, D_MODEL), lambda i: (tile(i), 0)),
        pl.BlockSpec((1, D_MODEL), lambda i: (0, 0)),
        pl.BlockSpec((D_MODEL, LANES), lambda i: (0, 0)),
        one, one,
    ]
    args = [x, g.reshape(1, D_MODEL), w_all, b_all, base]
    if h_into is None:
        steps, h_rows, aliases = n_tiles + 1, (n_tiles + 1) * tm, {}
    else:
        steps, h_rows, aliases = n_tiles, h_into.shape[0], {len(args): 0}
        in_specs.append(pl.BlockSpec(memory_space=pl.ANY))
        args.append(h_into)
    return pl.pallas_call(
        functools.partial(_router_kernel, n_tiles=n_tiles),
        grid=(steps,),
        in_specs=in_specs,
        out_specs=[pl.BlockSpec((tm, D_MODEL), lambda i: (h_row0 // tm + i, 0)),
                   pl.BlockSpec((META_ROWS, tm), lambda i: (0, tile(i))),
                   pl.BlockSpec((tm, LANES), lambda i: (tile(i), 0)), one],
        out_shape=[
            jax.ShapeDtypeStruct((h_rows, D_MODEL), F32),
            jax.ShapeDtypeStruct((META_ROWS, m), jnp.int32),
            jax.ShapeDtypeStruct((m, LANES), F32),
            jax.ShapeDtypeStruct((1, LANES), F32),
        ],
        input_output_aliases=aliases,
        compiler_params=_cparams(("arbitrary",)),
        name="moe_router",
    )(*args)


def _gather_rows(idx_ref, base, n_rows, src_hbm, dst, sem):
    for r in range(n_rows):
        t = idx_ref[base + r]
        pltpu.make_async_copy(src_hbm.at[pl.ds(t, 1)], dst.at[pl.ds(r, 1)], sem).start()


def _wait_rows(n_rows, src_hbm, dst, sem):
    pltpu.make_async_copy(src_hbm.at[pl.ds(0, n_rows)], dst, sem).wait()


def _moe_ffn_kernel(be_ref, nu_ref, nv_ref, tok_ref, h_hbm, wg_ref, wu_ref, wd_ref, y_ref, xbuf, sem):
    i = pl.program_id(0)
    n_used = nu_ref[0]
    n_groups = MOE_ROWS // GATHER_ROWS

    def fetch(block):
        slot = block % 2
        for grp in range(n_groups):
            @pl.when(grp * GATHER_ROWS < nv_ref[block])
            def _():
                _gather_rows(tok_ref, block * MOE_ROWS + grp * GATHER_ROWS, GATHER_ROWS, h_hbm,
                             xbuf.at[slot, pl.ds(grp * GATHER_ROWS, GATHER_ROWS)], sem.at[slot, grp])

    @pl.when(i == 0)
    def _():
        xbuf[...] = jnp.zeros_like(xbuf)
        fetch(0)

    @pl.when(i + 1 < n_used)
    def _():
        fetch(i + 1)

    @pl.when(i < n_used)
    def _():
        slot = i % 2
        for grp in range(n_groups):
            @pl.when(grp * GATHER_ROWS < nv_ref[i])
            def _():
                _wait_rows(GATHER_ROWS, h_hbm, xbuf.at[slot, pl.ds(grp * GATHER_ROWS, GATHER_ROWS)],
                           sem.at[slot, grp])
        x = xbuf[slot].astype(BF16)
        gate = _dot(x, wg_ref[0, 0].astype(BF16))
        up = _dot(x, wu_ref[0, 0].astype(BF16))
        act = (gate * jax.nn.sigmoid(gate) * up).astype(BF16)
        y_ref[...] = _dot(act, wd_ref[0, 0].astype(BF16))

    @pl.when(i >= n_used)
    def _():
        y_ref[...] = jnp.zeros_like(y_ref)


def _moe_ffn(h_all, plan, w_gate, w_up, w_down, layer):
    blk_expert, n_used, n_valid, tok_of_slot = plan
    n_blocks = tok_of_slot.shape[0] // MOE_ROWS
    rows = lambda i, be, nu, nv, tok: (i, 0)
    wmap = lambda i, be, nu, nv, tok: (layer, be[i], 0, 0)
    return pl.pallas_call(
        _moe_ffn_kernel,
        grid_spec=pltpu.PrefetchScalarGridSpec(
            num_scalar_prefetch=4,
            grid=(n_blocks,),
            in_specs=[
                pl.BlockSpec(memory_space=pl.ANY),
                pl.BlockSpec((1, 1, D_MODEL, E_FF), wmap),
                pl.BlockSpec((1, 1, D_MODEL, E_FF), wmap),
                pl.BlockSpec((1, 1, E_FF, D_MODEL), wmap),
            ],
            out_specs=pl.BlockSpec((MOE_ROWS, D_MODEL), rows),
            scratch_shapes=[pltpu.VMEM((2, MOE_ROWS, D_MODEL), F32),
                            pltpu.SemaphoreType.DMA((2, MOE_ROWS // GATHER_ROWS))],
        ),
        out_shape=jax.ShapeDtypeStruct((n_blocks * MOE_ROWS, D_MODEL), F32),
        compiler_params=_cparams(("arbitrary",), unchecked_dma=True),
        name="moe_ffn",
    )(blk_expert, n_used, n_valid, tok_of_slot, h_all, w_gate, w_up, w_down)


def _moe_plan_kernel(meta_ref, cnt_ref, be_ref, nu_ref, nv_ref, tok_ref, dp_ref, ds_ref, start_scr,
                     *, n_p, n_s, n_blocks):
    n_tok = n_p + n_s

    def segment(e, first_blk):
        start_scr[e] = first_blk * MOE_ROWS
        cnt = cnt_ref[e]
        nb = (cnt + MOE_ROWS - 1) // MOE_ROWS

        def fill(j, carry):
            be_ref[first_blk + j] = e
            nv_ref[first_blk + j] = jnp.minimum(cnt - j * MOE_ROWS, MOE_ROWS)
            return carry
        lax.fori_loop(0, nb, fill, 0)
        return first_blk + nb
    n_used = lax.fori_loop(0, N_EXPERTS, segment, 0)
    nu_ref[0] = n_used

    last_expert = be_ref[jnp.maximum(n_used - 1, 0)]

    def tail(j, carry):
        be_ref[j] = last_expert
        nv_ref[j] = 0
        return carry
    lax.fori_loop(n_used, n_blocks, tail, 0)

    def clear(j, carry):
        tok_ref[j] = 0
        return carry
    lax.fori_loop(0, n_blocks * MOE_ROWS, clear, 0, unroll=8)

    def place(dest_ref, first, count):
        def body(j, carry):
            t = first + j
            for k in range(2):
                d = start_scr[meta_ref[k * n_tok + t]] + meta_ref[(2 + k) * n_tok + t]
                tok_ref[d] = t
                dest_ref[k * count + j] = d
            return carry
        lax.fori_loop(0, count, body, 0, unroll=4)
    place(dp_ref, 0, n_p)
    place(ds_ref, n_p, n_s)


def _moe_plan(meta, counts, *, n_p, n_s):
    n_blocks = (2 * (n_p + n_s)) // MOE_ROWS + N_EXPERTS
    smem = pl.BlockSpec(memory_space=pltpu.SMEM)
    i32 = lambda n: jax.ShapeDtypeStruct((n,), jnp.int32)
    be, nu, nv, tok, dp, ds = pl.pallas_call(
        functools.partial(_moe_plan_kernel, n_p=n_p, n_s=n_s, n_blocks=n_blocks),
        in_specs=[smem, smem],
        out_specs=[smem] * 6,
        out_shape=[i32(n_blocks), i32(1), i32(n_blocks), i32(n_blocks * MOE_ROWS), i32(2 * n_p), i32(2 * n_s)],
        scratch_shapes=[pltpu.SMEM((N_EXPERTS,), jnp.int32)],
        name="moe_plan",
    )(meta, counts)
    return (be, nu, nv, tok), dp, ds


def _moe_combine_kernel(dest_ref, x_ref, g_ref, y_hbm, o_ref, ybuf, sem, *, tm, n_tiles):
    i = pl.program_id(0)

    def fetch(tile):
        slot = tile % 2
        for k in range(2):
            _gather_rows(dest_ref, (k * n_tiles + tile) * tm, tm, y_hbm, ybuf.at[slot, k], sem.at[slot, k])

    @pl.when(i == 0)
    def _():
        fetch(0)

    @pl.when(i + 1 < n_tiles)
    def _():
        fetch(i + 1)

    slot = i % 2
    acc = x_ref[...]
    for k in range(2):
        _wait_rows(tm, y_hbm, ybuf.at[slot, k], sem.at[slot, k])
        acc = acc + g_ref[:, k:k + 1] * ybuf[slot, k]
    o_ref[...] = acc


def _moe_combine(x, gates, dest, y_rows, *, tm):
    m = x.shape[0]
    n_tiles = m // tm
    return pl.pallas_call(
        functools.partial(_moe_combine_kernel, tm=tm, n_tiles=n_tiles),
        grid_spec=pltpu.PrefetchScalarGridSpec(
            num_scalar_prefetch=1,
            grid=(n_tiles,),
            in_specs=[
                pl.BlockSpec((tm, D_MODEL), lambda i, d: (i, 0)),
                pl.BlockSpec((tm, LANES), lambda i, d: (i, 0)),
                pl.BlockSpec(memory_space=pl.ANY),
            ],
            out_specs=pl.BlockSpec((tm, D_MODEL), lambda i, d: (i, 0)),
            scratch_shapes=[pltpu.VMEM((2, 2, tm, D_MODEL), F32), pltpu.SemaphoreType.DMA((2, 2))],
        ),
        out_shape=jax.ShapeDtypeStruct((m, D_MODEL), F32),
        compiler_params=_cparams(("arbitrary",), unchecked_dma=True),
        name="moe_combine",
    )(dest, x, gates, y_rows)


def _moe(xp, xs, g, w_rg, b_rg, w_re, b_re, w_gate, w_up, w_down, layer):
    n_p, n_s = xp.shape[0], xs.shape[0]
    pad = LANES - E_GROUPS - N_EXPERTS
    w_all = jnp.pad(jnp.concatenate([w_rg, w_re.reshape(D_MODEL, N_EXPERTS)], axis=1), ((0, 0), (0, pad)))
    b_all = jnp.pad(jnp.concatenate([b_rg, b_re.reshape(N_EXPERTS)]), (0, pad)).reshape(1, LANES)
    h_all, mip, mfp, cnt_p = _router(xp, g, w_all, b_all, jnp.zeros((1, LANES), F32), tm=512)
    h_all, mis, mfs, cnt = _router(xs, g, w_all, b_all, cnt_p, tm=n_s, h_into=h_all, h_row0=n_p)
    meta = jnp.concatenate([mip[:4], mis[:4]], axis=1).reshape(-1)
    plan, dest_p, dest_s = _moe_plan(meta, cnt[0].astype(jnp.int32), n_p=n_p, n_s=n_s)
    y_rows = _moe_ffn(h_all, plan, w_gate, w_up, w_down, layer)
    return (_moe_combine(xp, mfp, dest_p, y_rows, tm=COMBINE_ROWS),
            _moe_combine(xs, mfs, dest_s, y_rows, tm=n_s))


def _final_norm_kernel(x_ref, g_ref, o_ref):
    o_ref[...] = _rmsnorm_rows(x_ref[...], g_ref[...])


def _final_norm(x, g, *, tm):
    m = x.shape[0]
    return pl.pallas_call(
        _final_norm_kernel,
        grid=(m // tm,),
        in_specs=[pl.BlockSpec((tm, D_MODEL), lambda i: (i, 0)), pl.BlockSpec((1, D_MODEL), lambda i: (0, 0))],
        out_specs=pl.BlockSpec((tm, D_MODEL), lambda i: (i, 0)),
        out_shape=jax.ShapeDtypeStruct((m, D_MODEL), F32),
        compiler_params=_cparams(("parallel",)),
        name="final_norm",
    )(x, g.reshape(1, D_MODEL))


def _rope_tables(pos):
    half = HEAD_DIM // 2
    inv = 1.0 / (ROPE_THETA ** (jnp.arange(half, dtype=F32) / half))
    ang = pos.astype(F32)[:, None] * inv[None, :]
    cos, sin = jnp.cos(ang), jnp.sin(ang)
    return jnp.concatenate([cos, cos], -1), jnp.concatenate([-sin, sin], -1)


def _kv_state(proj3, group, keep):
    b, t, _ = proj3.shape
    base = 2 * A_WIDTH + group * 3 * B_HW
    k = proj3[:, t - keep:, base + B_HW: base + 2 * B_HW]
    v = proj3[:, t - keep:, base + 2 * B_HW: base + 3 * B_HW]
    return jnp.stack([k, v], axis=2).reshape(b, keep, 2, B_HEADS, HEAD_DIM)


def _layer_ab(xp, xs, g, w_in_b, ln_g, ln_b, w_s, b_s, w_out_b, cache_views, layer, tabs_p, tabs_s, shapes):
    bp, tp, bs, ts = shapes

    proj_p = _inproj(xp, g, w_in_b, layer, tabs_p, tm=1024)
    a_p = _gmlp(proj_p, ln_g, ln_b, w_s, b_s, n_rows=bp * tp, chunks=2, emit_v=False)[0]
    outs, lses = zip(*[_attn_prompt(proj_p, grp, batch=bp, seq=tp) for grp in range(len(B_CONFIGS))])
    b_p = _merge(outs, lses, tm=1024)
    xp_new = _outproj([a_p, b_p], w_out_b, layer, xp, tm=1024, tn=1024)
    proj_p3 = proj_p.reshape(bp, tp, AB_IN)
    kv_p = [_kv_state(proj_p3, grp, min(w, tp)) for grp, (w, _) in enumerate(B_CONFIGS)]

    n_s = bs * ts
    proj_s = _inproj(xs, g, w_in_b, layer, tabs_s, tm=n_s)
    proj_s3 = proj_s.reshape(bs, ts, AB_IN)
    chunk_in = jnp.pad(proj_s3[:, :, :2 * A_WIDTH], ((0, 0), (0, A_CHUNK - ts), (0, 0)))
    a_s, v_s = _gmlp(chunk_in.reshape(bs * A_CHUNK, 2 * A_WIDTH), ln_g, ln_b, w_s, b_s,
                     n_rows=bs * A_CHUNK, chunks=1, emit_v=True)
    a_s = a_s.reshape(bs, A_CHUNK, A_WIDTH)[:, :ts].reshape(n_s, A_WIDTH)
    v_s = v_s.reshape(bs, A_CHUNK, A_WIDTH)[:, :ts]
    qkv_s = jnp.pad(proj_s3[:, :, 2 * A_WIDTH:], ((0, 0), (0, SAMPLE_ROWS - ts), (0, 0)))
    b_s_out = _attn_sample(qkv_s, cache_views, layer)[:, :ts].reshape(n_s, B_HW)
    xs_new = _outproj([a_s, b_s_out], w_out_b, layer, xs, tm=n_s, tn=1024)
    kv_s = [_kv_state(proj_s3, grp, ts) for grp in range(len(B_CONFIGS))]
    return xp_new, xs_new, kv_p, kv_s, v_s


def _layer_c(xp, xs, g, w_main, w_dt, conv_w, conv_b, dt_bias, a_log, d_skip, norm_g, w_out_b, layer,
             conv_state, ssm_state, shapes):
    bp, tp, bs, ts = shapes
    n_main = C_D_INNER + C_CONV_DIM
    args = (conv_w, conv_b, dt_bias, a_log, d_skip, norm_g)

    proj_p = _inproj(xp, g, w_main, layer, tm=1024, n_out=n_main)
    dt_p = _inproj(xp, g, w_dt, layer, tm=1024)
    zc = jnp.zeros((bp, CONV_PAD, C_CONV_DIM), F32)
    zh = jnp.zeros((bp, C_GROUPS, C_D_STATE, C_GW), F32)
    q_p = 128
    y_p, h_p = _ssd(proj_p, dt_p, zc, zh, *args, batch=bp, n_chunks=tp // q_p, q_len=q_p, t_valid=q_p)
    xp_new = _outproj([y_p], w_out_b, layer, xp, tm=512, tn=1024)
    conv_p = proj_p.reshape(bp, tp, n_main)[:, tp - (C_CONV - 1):, C_D_INNER:]

    n_s = bs * ts
    q_s = 16
    proj_s = _inproj(xs, g, w_main, layer, tm=n_s, n_out=n_main)
    dt_s = _inproj(xs, g, w_dt, layer, tm=n_s)
    pad_rows = lambda a: jnp.pad(a.reshape(bs, ts, -1), ((0, 0), (0, q_s - ts), (0, 0))).reshape(bs * q_s, -1)
    cst = jnp.pad(conv_state, ((0, 0), (CONV_PAD - (C_CONV - 1), 0), (0, 0)))
    y_s, h_s = _ssd(pad_rows(proj_s), pad_rows(dt_s), cst, _state_to_t(ssm_state), *args,
                    batch=bs, n_chunks=1, q_len=q_s, t_valid=ts)
    y_s = y_s.reshape(bs, q_s, C_D_INNER)[:, :ts].reshape(n_s, C_D_INNER)
    xs_new = _outproj([y_s], w_out_b, layer, xs, tm=n_s, tn=1024)
    assert ts >= C_CONV - 1
    conv_s = proj_s.reshape(bs, ts, n_main)[:, ts - (C_CONV - 1):, C_D_INNER:]
    return xp_new, xs_new, conv_p, conv_s, _state_from_t(h_p), _state_from_t(h_s)


def kernel(x_prompt, x_sample, cache_kv_w128, cache_kv_w512, cache_kv_w2048, state_conv, state_ssm,
           norm_mix, norm_ffn, norm_final, w_in_ab, a_ln_g, a_ln_b, a_w_s, a_b_s, w_out_ab,
           w_in_c, c_conv_w, c_conv_b, c_dt_bias, c_a_log, c_d, c_norm_g, w_out_c,
           w_router_g, b_router_g, w_router_e, b_router_e, w_exp_gate, w_exp_up, w_exp_down):
    bp, tp, _ = x_prompt.shape
    bs, ts, _ = x_sample.shape
    shapes = (bp, tp, bs, ts)
    depth = norm_mix.shape[0]
    kv_caches = (cache_kv_w128, cache_kv_w512, cache_kv_w2048)
    tabs_p = _rope_tables(jnp.arange(tp, dtype=jnp.int32))
    pos_s = PAST_LEN + jnp.arange(ts, dtype=jnp.int32)
    tabs_s = tuple(jnp.tile(t, (bs, 1)) for t in _rope_tables(pos_s))

    w_in_ab_b = w_in_ab.astype(BF16)
    w_out_ab_b = w_out_ab.astype(BF16)
    w_in_c_b = w_in_c[:, :, :C_D_INNER + C_CONV_DIM].astype(BF16)
    w_dt_b = jnp.pad(w_in_c[:, :, C_D_INNER + C_CONV_DIM:], ((0, 0), (0, 0), (0, LANES - C_HEADS))).astype(BF16)
    w_out_c_b = w_out_c.astype(BF16)
    cache_views = _cache_views(kv_caches, ts)

    xp = x_prompt.reshape(bp * tp, D_MODEL)
    xs = x_sample.reshape(bs * ts, D_MODEL)
    kv_p = [[] for _ in B_CONFIGS]
    kv_s = [[] for _ in B_CONFIGS]
    chunk_v, conv_p, conv_s, ssm_p, ssm_s = [], [], [], [], []
    for l in range(depth):
        i = l // 2
        if l % 2 == 0:
            xp, xs, nkv_p, nkv_s, v_s = _layer_ab(
                xp, xs, norm_mix[l], w_in_ab_b, a_ln_g[i], a_ln_b[i], a_w_s[i], a_b_s[i], w_out_ab_b,
                cache_views, i, tabs_p, tabs_s, shapes)
            for grp in range(len(B_CONFIGS)):
                kv_p[grp].append(nkv_p[grp])
                kv_s[grp].append(nkv_s[grp])
            chunk_v.append(v_s)
        else:
            xp, xs, ncp, ncs, nsp, nss = _layer_c(
                xp, xs, norm_mix[l], w_in_c_b, w_dt_b, c_conv_w[i], c_conv_b[i], c_dt_bias[i], c_a_log[i], c_d[i],
                c_norm_g[i], w_out_c_b, i, state_conv[i], state_ssm[i], shapes)
            conv_p.append(ncp)
            conv_s.append(ncs)
            ssm_p.append(nsp)
            ssm_s.append(nss)
        xp, xs = _moe(xp, xs, norm_ffn[l], w_router_g[l], b_router_g[l], w_router_e[l], b_router_e[l],
                      w_exp_gate, w_exp_up, w_exp_down, l)
    y_p = _final_norm(xp, norm_final, tm=1024).reshape(bp, tp, D_MODEL)
    y_s = _final_norm(xs, norm_final, tm=bs * ts).reshape(bs, ts, D_MODEL)
    kv_s = [jnp.concatenate([c[:, :, ts:], jnp.stack(new)], axis=2) for c, new in zip(kv_caches, kv_s)]
    return (y_p, y_s,
            jnp.stack(kv_p[0]), kv_s[0],
            jnp.stack(kv_p[1]), kv_s[1],
            jnp.stack(kv_p[2]), kv_s[2],
            jnp.stack(chunk_v),
            jnp.stack(conv_p), jnp.stack(conv_s),
            jnp.stack(ssm_p), jnp.stack(ssm_s))
```

```python
import functools
import math

import jax
import jax.numpy as jnp
from jax import lax
from jax.experimental import pallas as pl
from jax.experimental.pallas import tpu as pltpu

F32 = jnp.float32
BF16 = jnp.bfloat16

D_MODEL = 2048
PAST_LEN = 16384
EPS = 1e-6
A_WIDTH = 1024
A_GROUPS = 8
A_GW = 128
A_CHUNK = 128
HEAD_DIM = 128
B_HEADS = 4
B_CONFIGS = ((128, 1), (512, 4), (2048, 16))
B_BAND = 128
B_HW = B_HEADS * HEAD_DIM
AB_IN = 2 * A_WIDTH + 3 * len(B_CONFIGS) * B_HW
ROPE_THETA = 10000.0
C_D_INNER = 4096
C_HEAD_DIM = 64
C_HEADS = 64
C_GROUPS = 8
C_D_STATE = 128
C_CONV = 4
C_CONV_DIM = C_D_INNER + 2 * C_GROUPS * C_D_STATE
C_GW = C_D_INNER // C_GROUPS
E_GROUPS = 8
E_PER_GROUP = 8
N_EXPERTS = 64
E_FF = 512
MOE_ROWS = 256
GATHER_ROWS = 32
COMBINE_ROWS = 128
META_ROWS = 8

LANES = 128
VMEM_LIMIT = 56 * 1024 * 1024


def _cparams(sem, unchecked_dma=False):
    return pltpu.CompilerParams(dimension_semantics=sem, vmem_limit_bytes=VMEM_LIMIT,
                                disable_bounds_checks=unchecked_dma)


def _split3(x):
    hi = x.astype(BF16)
    r1 = x - hi.astype(F32)
    mid = r1.astype(BF16)
    lo = (r1 - mid.astype(F32)).astype(BF16)
    return hi, mid, lo


def _dot(a, b):
    return jnp.dot(a, b, preferred_element_type=F32)


def _dot_nt(a, b):
    return lax.dot_general(a, b, (((1,), (1,)), ((), ())), preferred_element_type=F32)


def _rmsnorm_rows(x, g):
    ms = jnp.mean(x * x, axis=-1, keepdims=True)
    return x * lax.rsqrt(ms + EPS) * g


def _inproj_kernel(x_ref, g_ref, w_ref, *rest, rope):
    o_ref, h_scr = rest[-2:]
    j = pl.program_id(1)

    @pl.when(j == 0)
    def _():
        h_scr[...] = _rmsnorm_rows(x_ref[...], g_ref[...]).astype(BF16)

    acc = _dot(h_scr[...], w_ref[0])
    if not rope:
        o_ref[...] = acc
        return
    cos_ref, sin_ref = rest[:2]
    is_qk = jnp.logical_and(j >= 4, (j - 4) % 3 != 2)

    @pl.when(is_qk)
    def _():
        cos = cos_ref[...]
        sin = sin_ref[...]
        for h in range(B_HEADS):
            sl = slice(h * HEAD_DIM, (h + 1) * HEAD_DIM)
            a = acc[:, sl]
            o_ref[:, sl] = a * cos + pltpu.roll(a, HEAD_DIM // 2, 1) * sin

    @pl.when(jnp.logical_not(is_qk))
    def _():
        o_ref[...] = acc


def _inproj(x, g, w_bf16, layer, tabs=None, *, tm, n_out=None):
    m, _ = x.shape
    n = n_out or w_bf16.shape[2]
    tn = B_HW if n % B_HW == 0 else n
    in_specs = [
        pl.BlockSpec((tm, D_MODEL), lambda i, j: (i, 0)),
        pl.BlockSpec((1, D_MODEL), lambda i, j: (0, 0)),
        pl.BlockSpec((1, D_MODEL, tn), lambda i, j: (layer, 0, j)),
    ]
    args = [x, g.reshape(1, D_MODEL), w_bf16]
    if tabs is not None:
        n_pos_tiles = tabs[0].shape[0] // tm
        in_specs += [pl.BlockSpec((tm, HEAD_DIM), lambda i, j: (i % n_pos_tiles, 0))] * 2
        args += list(tabs)
    return pl.pallas_call(
        functools.partial(_inproj_kernel, rope=tabs is not None),
        grid=(m // tm, n // tn),
        in_specs=in_specs,
        out_specs=pl.BlockSpec((tm, tn), lambda i, j: (i, j)),
        out_shape=jax.ShapeDtypeStruct((m, n), F32),
        scratch_shapes=[pltpu.VMEM((tm, D_MODEL), BF16)],
        compiler_params=_cparams(("parallel", "arbitrary")),
        name="inproj",
    )(*args)


def _gmlp_kernel(p_ref, lng_ref, lnb_ref, ws_ref, bs_ref, a_ref, *v_ref, chunks):
    row = lax.broadcasted_iota(jnp.int32, (A_CHUNK, A_CHUNK), 0)
    col = lax.broadcasted_iota(jnp.int32, (A_CHUNK, A_CHUNK), 1)
    causal = col <= row
    for c in range(chunks):
        rows = slice(c * A_CHUNK, (c + 1) * A_CHUNK)
        u = jax.nn.gelu(p_ref[rows, 0:A_WIDTH], approximate=True)
        vg = jax.nn.gelu(p_ref[rows, A_WIDTH:2 * A_WIDTH], approximate=True)
        vc = vg - jnp.mean(vg, axis=-1, keepdims=True)
        v = vc * lax.rsqrt(jnp.mean(vc * vc, axis=-1, keepdims=True) + EPS)
        v = v * lng_ref[...] + lnb_ref[...]
        if v_ref:
            v_ref[0][rows, :] = v
        vb = v.astype(BF16)
        for g in range(A_GROUPS):
            sl = slice(g * A_GW, (g + 1) * A_GW)
            w = jnp.where(causal, ws_ref[g], 0.0).astype(BF16)
            mixed = _dot(w, vb[:, sl]) + bs_ref[:, sl]
            a_ref[rows, sl] = (u[:, sl] * mixed).astype(BF16)


def _gmlp(proj, ln_g, ln_b, w_s, b_s, *, n_rows, chunks, emit_v):
    rows = chunks * A_CHUNK
    bsb = jnp.repeat(jnp.transpose(b_s), A_GW, axis=1)
    out_shape = [jax.ShapeDtypeStruct((n_rows, A_WIDTH), BF16)]
    out_specs = [pl.BlockSpec((rows, A_WIDTH), lambda i: (i, 0))]
    if emit_v:
        out_shape.append(jax.ShapeDtypeStruct((n_rows, A_WIDTH), F32))
        out_specs.append(pl.BlockSpec((rows, A_WIDTH), lambda i: (i, 0)))
    return pl.pallas_call(
        functools.partial(_gmlp_kernel, chunks=chunks),
        grid=(n_rows // rows,),
        in_specs=[
            pl.BlockSpec((rows, 2 * A_WIDTH), lambda i: (i, 0)),
            pl.BlockSpec((1, A_WIDTH), lambda i: (0, 0)),
            pl.BlockSpec((1, A_WIDTH), lambda i: (0, 0)),
            pl.BlockSpec((A_GROUPS, A_CHUNK, A_CHUNK), lambda i: (0, 0, 0)),
            pl.BlockSpec((A_CHUNK, A_WIDTH), lambda i: (0, 0)),
        ],
        out_specs=out_specs,
        out_shape=out_shape,
        compiler_params=_cparams(("parallel",)),
        name="gmlp",
    )(proj, ln_g.reshape(1, A_WIDTH), ln_b.reshape(1, A_WIDTH), w_s, bsb)


def _attn_prompt_kernel(q_ref, kp_ref, kc_ref, vp_ref, vc_ref, o_ref, lse_ref, *, dil, heads):
    n = pl.program_id(2)
    qi = lax.broadcasted_iota(jnp.int32, (B_BAND, 2 * B_BAND), 0)
    kc_idx = lax.broadcasted_iota(jnp.int32, (B_BAND, 2 * B_BAND), 1)
    dist = qi + B_BAND - kc_idx
    ok = jnp.logical_and(jnp.logical_and(dist >= 0, dist <= B_BAND),
                         jnp.logical_or(kc_idx >= B_BAND, n > 0))
    scale = HEAD_DIM ** -0.5
    for r in range(dil):
        rows = pl.ds(r, B_BAND, stride=dil) if dil > 1 else slice(None)
        for h in range(heads):
            sl = slice(h * HEAD_DIM, (h + 1) * HEAD_DIM)
            q = q_ref[rows, sl].astype(BF16)
            k = jnp.concatenate([kp_ref[rows, sl], kc_ref[rows, sl]], axis=0).astype(BF16)
            v = jnp.concatenate([vp_ref[rows, sl], vc_ref[rows, sl]], axis=0).astype(BF16)
            s = jnp.where(ok, _dot_nt(q, k) * scale, -jnp.inf)
            m = jnp.max(s, axis=-1, keepdims=True)
            p = jnp.exp(s - m)
            l = jnp.sum(p, axis=-1, keepdims=True)
            o_ref[rows, sl] = _dot(p.astype(BF16), v) / l
            lse_ref[rows, sl] = jnp.broadcast_to(m + jnp.log(l), (B_BAND, HEAD_DIM))


def _attn_prompt(proj, group, *, batch, seq):
    dil = B_CONFIGS[group][1]
    rows = B_BAND * dil
    nblk = seq // rows
    heads = B_HEADS if dil == 1 else 1
    width = heads * HEAD_DIM
    n_col = B_HW // width
    qcol = (4 + 3 * group) * n_col

    def spec(off, prev):
        def imap(b, h, n):
            nn = jnp.maximum(n - 1, 0) if prev else n
            return (b * nblk + nn, qcol + off * n_col + h)
        return pl.BlockSpec((rows, width), imap)

    out_spec = pl.BlockSpec((rows, width), lambda b, h, n: (b * nblk + n, h))
    return pl.pallas_call(
        functools.partial(_attn_prompt_kernel, dil=dil, heads=heads),
        grid=(batch, n_col, nblk),
        in_specs=[spec(0, False), spec(1, True), spec(1, False), spec(2, True), spec(2, False)],
        out_specs=[out_spec, out_spec],
        out_shape=[jax.ShapeDtypeStruct((batch * seq, B_HW), F32)] * 2,
        compiler_params=_cparams(("parallel", "parallel", "arbitrary")),
        name=f"attn_prompt_g{group}",
    )(proj, proj, proj, proj, proj)


def _merge_kernel(o0, o1, o2, l0, l1, l2, b_ref):
    m = jnp.maximum(jnp.maximum(l0[...], l1[...]), l2[...])
    e0 = jnp.exp(l0[...] - m)
    e1 = jnp.exp(l1[...] - m)
    e2 = jnp.exp(l2[...] - m)
    den = e0 + e1 + e2
    b_ref[...] = ((e0 / den) * o0[...] + (e1 / den) * o1[...] + (e2 / den) * o2[...]).astype(BF16)


def _merge(outs, lses, *, tm):
    m = outs[0].shape[0]
    spec = pl.BlockSpec((tm, B_HW), lambda i: (i, 0))
    return pl.pallas_call(
        _merge_kernel,
        grid=(m // tm,),
        in_specs=[spec] * 6,
        out_specs=spec,
        out_shape=jax.ShapeDtypeStruct((m, B_HW), BF16),
        compiler_params=_cparams(("parallel",)),
        name="attn_merge",
    )(*outs, *lses)


SAMPLE_ROWS = 16


def _attn_sample_kernel(qkv_ref, c0_ref, c1_ref, c2_ref, b_ref, *, n_real):
    caches = (c0_ref, c1_ref, c2_ref)
    row_w = 2 * B_HW
    scale = HEAD_DIM ** -0.5
    qi_c = lax.broadcasted_iota(jnp.int32, (SAMPLE_ROWS, B_BAND), 0)
    ka_c = lax.broadcasted_iota(jnp.int32, (SAMPLE_ROWS, B_BAND), 1)
    qi_n = lax.broadcasted_iota(jnp.int32, (SAMPLE_ROWS, SAMPLE_ROWS), 0)
    km_n = lax.broadcasted_iota(jnp.int32, (SAMPLE_ROWS, SAMPLE_ROWS), 1)

    def valid(diff, dil):
        shift = dil.bit_length() - 1
        ok = jnp.logical_and(diff >= 0, (diff & (dil - 1)) == 0)
        return jnp.logical_and(ok, (diff >> shift) <= B_BAND)

    for h in range(B_HEADS):
        hs = slice(h * HEAD_DIM, (h + 1) * HEAD_DIM)
        outs, lses = [], []
        for g, (window, dil) in enumerate(B_CONFIGS):
            base = g * 3 * B_HW
            q = qkv_ref[0, :, base + h * HEAD_DIM: base + (h + 1) * HEAD_DIM].astype(BF16)
            kn = qkv_ref[0, :, base + B_HW + h * HEAD_DIM: base + B_HW + (h + 1) * HEAD_DIM].astype(BF16)
            vn = qkv_ref[0, :, base + 2 * B_HW + h * HEAD_DIM: base + 2 * B_HW + (h + 1) * HEAD_DIM].astype(BF16)
            n_res = min(dil, n_real)
            scores, values = [], []
            for r in range(n_res):
                kc = caches[g][0, 0, :, r * row_w + h * HEAD_DIM: r * row_w + (h + 1) * HEAD_DIM].astype(BF16)
                vc = caches[g][0, 0, :, r * row_w + B_HW + h * HEAD_DIM: r * row_w + B_HW + (h + 1) * HEAD_DIM].astype(BF16)
                s = _dot_nt(q, kc) * scale
                diff = qi_c + window - ka_c * dil - r
                scores.append(jnp.where(valid(diff, dil), s, -jnp.inf))
                values.append(vc)
            s = _dot_nt(q, kn) * scale
            scores.append(jnp.where(valid(qi_n - km_n, dil), s, -jnp.inf))
            values.append(vn)
            m = functools.reduce(jnp.maximum, [jnp.max(s, axis=-1, keepdims=True) for s in scores])
            ps = [jnp.exp(s - m) for s in scores]
            l = functools.reduce(jnp.add, [jnp.sum(p, axis=-1, keepdims=True) for p in ps])
            o = functools.reduce(jnp.add, [_dot(p.astype(BF16), v) for p, v in zip(ps, values)])
            outs.append(o / l)
            lses.append(m + jnp.log(l))
        m = functools.reduce(jnp.maximum, lses)
        es = [jnp.exp(l - m) for l in lses]
        den = functools.reduce(jnp.add, es)
        b_ref[0, :, hs] = functools.reduce(jnp.add, [(e / den) * o for e, o in zip(es, outs)]).astype(BF16)


def _cache_views(caches, n_real):
    views = []
    for (window, dil), c in zip(B_CONFIGS, caches):
        layers, b = c.shape[:2]
        assert c.shape[2] == window and window == dil * B_BAND
        n_res = min(dil, n_real)
        c = c.reshape(layers, b, B_BAND, dil, 2, B_HEADS, HEAD_DIM)[:, :, :, :n_res]
        views.append(c.reshape(layers, b, B_BAND, n_res * 2 * B_HW))
    return views


def _attn_sample(qkv, views, layer):
    b = qkv.shape[0]
    n_real = max(v.shape[3] // (2 * B_HW) for v in views)
    specs = [pl.BlockSpec((1, 1) + v.shape[2:], lambda i: (layer, i, 0, 0)) for v in views]
    return pl.pallas_call(
        functools.partial(_attn_sample_kernel, n_real=n_real),
        grid=(b,),
        in_specs=[pl.BlockSpec((1, SAMPLE_ROWS, qkv.shape[2]), lambda i: (i, 0, 0))] + specs,
        out_specs=pl.BlockSpec((1, SAMPLE_ROWS, B_HW), lambda i: (i, 0, 0)),
        out_shape=jax.ShapeDtypeStruct((b, SAMPLE_ROWS, B_HW), BF16),
        compiler_params=_cparams(("parallel",)),
        name="attn_sample",
    )(qkv, *views)


def _outproj_kernel(*refs, n_parts):
    a_refs = refs[:n_parts]
    w_refs = refs[n_parts:2 * n_parts]
    x_ref, o_ref = refs[2 * n_parts], refs[2 * n_parts + 1]
    acc = x_ref[...]
    for a, w in zip(a_refs, w_refs):
        acc = acc + _dot(a[...], w[0])
    o_ref[...] = acc


def _outproj(parts, w_bf16, layer, x, *, tm, tn):
    m, n = x.shape
    in_specs = [pl.BlockSpec((tm, a.shape[1]), lambda j, i: (i, 0)) for a in parts]
    row0 = 0
    for a in parts:
        width = a.shape[1]
        blk = row0 // width
        assert blk * width == row0
        in_specs.append(pl.BlockSpec((1, width, tn), lambda j, i, blk=blk: (layer, blk, j)))
        row0 += width
    assert row0 == w_bf16.shape[1]
    weights = [w_bf16] * len(parts)
    in_specs.append(pl.BlockSpec((tm, tn), lambda j, i: (i, j)))
    return pl.pallas_call(
        functools.partial(_outproj_kernel, n_parts=len(parts)),
        grid=(n // tn, m // tm),
        in_specs=in_specs,
        out_specs=pl.BlockSpec((tm, tn), lambda j, i: (i, j)),
        out_shape=jax.ShapeDtypeStruct((m, n), F32),
        compiler_params=_cparams(("parallel", "parallel")),
        name="outproj",
    )(*parts, *weights, x)


CONV_PAD = 8


def _ssd_kernel(z_ref, xa_ref, xb_ref, bc_ref, dtr_ref, dtrt_ref, cst_ref, cw_ref, cb_ref,
                dtb_ref, dtbc_ref, alog_ref, alogc_ref, dsk_ref, ng_ref, e_ref, h0_ref,
                y_ref, h_ref, ext_scr, act_scr, *, q_len, t_valid):
    c = pl.program_id(1)

    @pl.when(c == 0)
    def _():
        ext_scr[0:CONV_PAD, :] = cst_ref[0]
        h_ref[...] = h0_ref[...]

    half = C_CONV_DIM // 3
    ext_scr[CONV_PAD:CONV_PAD + q_len, 0:half] = xa_ref[...]
    ext_scr[CONV_PAD:CONV_PAD + q_len, half:2 * half] = xb_ref[...]
    ext_scr[CONV_PAD:CONV_PAD + q_len, 2 * half:3 * half] = bc_ref[...]
    for s in range(C_CONV_DIM // C_GW):
        sl = slice(s * C_GW, (s + 1) * C_GW)
        acc = cb_ref[:, sl]
        for j in range(C_CONV):
            lo = CONV_PAD - (C_CONV - 1) + j
            acc = acc + ext_scr[lo:lo + q_len, sl] * cw_ref[j:j + 1, sl]
        act_scr[:, sl] = acc * jax.nn.sigmoid(acc)
    ext_scr[0:CONV_PAD, :] = ext_scr[q_len:q_len + CONV_PAD, :]

    ti = lax.broadcasted_iota(jnp.int32, (q_len, q_len), 0)
    si = lax.broadcasted_iota(jnp.int32, (q_len, q_len), 1)
    tri = si <= ti
    tri_b = jnp.where(tri, 1.0, 0.0).astype(BF16)
    tri_tb = jnp.where(ti <= si, 1.0, 0.0).astype(BF16)

    dt = jax.nn.softplus(dtr_ref[...] + dtb_ref[...])
    dtt = jax.nn.softplus(dtrt_ref[...] + dtbc_ref[...])
    if t_valid < q_len:
        dt = jnp.where(lax.broadcasted_iota(jnp.int32, dt.shape, 0) < t_valid, dt, 0.0)
        dtt = jnp.where(lax.broadcasted_iota(jnp.int32, dtt.shape, 1) < t_valid, dtt, 0.0)
    da = dt * (-jnp.exp(alog_ref[...]))
    dat = dtt * (-jnp.exp(alogc_ref[...]))
    a_cum = functools.reduce(jnp.add, [_dot(tri_b, p) for p in reversed(_split3(da))])
    a_cumt = functools.reduce(jnp.add, [_dot(p, tri_tb) for p in reversed(_split3(dat))])
    dt_parts = _split3(dt)
    ac_parts = _split3(a_cum)

    lane_lo = lax.broadcasted_iota(jnp.int32, (q_len, LANES), 1) < C_HEAD_DIM

    for g in range(C_GROUPS):
        gs = slice(g * C_GW, (g + 1) * C_GW)
        e_g = e_ref[:, gs]
        dt_x = functools.reduce(jnp.add, [_dot(p, e_g) for p in reversed(dt_parts)])
        ac_x = functools.reduce(jnp.add, [_dot(p, e_g) for p in reversed(ac_parts)])
        xs = act_scr[:, gs]
        bm = act_scr[:, C_D_INNER + g * C_D_STATE: C_D_INNER + (g + 1) * C_D_STATE]
        cm = act_scr[:, C_D_INNER + C_GROUPS * C_D_STATE + g * C_D_STATE:
                     C_D_INNER + C_GROUPS * C_D_STATE + (g + 1) * C_D_STATE]
        bm_b = bm.astype(BF16)
        cm_b = cm.astype(BF16)
        xdt = xs * dt_x
        a_last = ac_x[q_len - 1:q_len, :]
        xdt_b = xdt.astype(BF16)
        xdt_end_b = (xdt * jnp.exp(a_last - ac_x)).astype(BF16)
        cbm = _dot_nt(cm_b, bm_b)

        h_prev = h_ref[0, g]
        y = _dot(cm_b, h_prev.astype(BF16)) * jnp.exp(ac_x)
        st = _dot(jnp.transpose(bm).astype(BF16), xdt_end_b)
        h_ref[0, g] = h_prev * jnp.exp(a_last) + st

        pairs = []
        for hp in range(C_GW // LANES):
            ms = []
            for hh in (2 * hp, 2 * hp + 1):
                hd = g * (C_GW // C_HEAD_DIM) + hh
                seg = a_cum[:, hd:hd + 1] - a_cumt[hd:hd + 1, :]
                decay = jnp.exp(jnp.where(tri, seg, -jnp.inf))
                ms.append((cbm * decay).astype(BF16))
            slab = xdt_b[:, hp * LANES:(hp + 1) * LANES]
            zero = jnp.zeros_like(slab)
            rhs = jnp.concatenate([jnp.where(lane_lo, slab, zero), jnp.where(lane_lo, zero, slab)], axis=0)
            pairs.append(_dot(jnp.concatenate(ms, axis=1), rhs))
        y = y + jnp.concatenate(pairs, axis=1)
        y = y + xs * dsk_ref[:, gs]
        zz = z_ref[:, gs]
        y = y * (zz * jax.nn.sigmoid(zz))
        y = y * lax.rsqrt(jnp.mean(y * y, axis=-1, keepdims=True) + EPS) * ng_ref[:, gs]
        y_ref[:, gs] = y.astype(BF16)


def _ssd(proj, dt_raw, conv_state, h0t, conv_w, conv_b, dt_bias, a_log, d_skip, norm_g,
         *, batch, n_chunks, q_len, t_valid):
    rows = batch * n_chunks * q_len
    dtrt = dt_raw.reshape(batch * n_chunks, q_len, LANES).transpose(0, 2, 1).reshape(-1, q_len)
    pad = LANES - C_HEADS
    col = lambda v: jnp.pad(v, (0, pad)).reshape(LANES, 1)
    row = lambda v: jnp.pad(v, (0, pad)).reshape(1, LANES)
    expand = jnp.repeat(jnp.eye(LANES, C_HEADS, dtype=BF16), C_HEAD_DIM, axis=1)
    cw = jnp.pad(conv_w, ((0, CONV_PAD - C_CONV), (0, 0)))
    xcol = C_D_INNER // (C_CONV_DIM // 3)
    const = lambda shape: pl.BlockSpec(shape, lambda b, c: (0,) * len(shape))
    blk = lambda width, j: pl.BlockSpec((q_len, width), lambda b, c: (b * n_chunks + c, j))
    y, h = pl.pallas_call(
        functools.partial(_ssd_kernel, q_len=q_len, t_valid=t_valid),
        grid=(batch, n_chunks),
        in_specs=[
            blk(C_D_INNER, 0), blk(C_CONV_DIM // 3, xcol), blk(C_CONV_DIM // 3, xcol + 1),
            blk(C_CONV_DIM // 3, xcol + 2), blk(LANES, 0),
            pl.BlockSpec((LANES, q_len), lambda b, c: (b * n_chunks + c, 0)),
            pl.BlockSpec((1, CONV_PAD, C_CONV_DIM), lambda b, c: (b, 0, 0)),
            const((CONV_PAD, C_CONV_DIM)), const((1, C_CONV_DIM)),
            const((1, LANES)), const((LANES, 1)), const((1, LANES)), const((LANES, 1)),
            const((1, C_D_INNER)), const((1, C_D_INNER)), const((LANES, C_D_INNER)),
            pl.BlockSpec((1, C_GROUPS, C_D_STATE, C_GW), lambda b, c: (b, 0, 0, 0)),
        ],
        out_specs=[
            pl.BlockSpec((q_len, C_D_INNER), lambda b, c: (b * n_chunks + c, 0)),
            pl.BlockSpec((1, C_GROUPS, C_D_STATE, C_GW), lambda b, c: (b, 0, 0, 0)),
        ],
        out_shape=[
            jax.ShapeDtypeStruct((rows, C_D_INNER), BF16),
            jax.ShapeDtypeStruct((batch, C_GROUPS, C_D_STATE, C_GW), F32),
        ],
        scratch_shapes=[
            pltpu.VMEM((q_len + CONV_PAD, C_CONV_DIM), F32),
            pltpu.VMEM((q_len, C_CONV_DIM), F32),
        ],
        compiler_params=_cparams(("parallel", "arbitrary")),
        name="ssd",
    )(proj, proj, proj, proj, dt_raw, dtrt, conv_state, cw, conv_b.reshape(1, -1),
      row(dt_bias), col(dt_bias), row(a_log), col(a_log),
      jnp.repeat(d_skip, C_HEAD_DIM).reshape(1, -1), norm_g.reshape(1, -1), expand, h0t)
    return y, h


def _state_to_t(h):
    b = h.shape[0]
    return h.reshape(b, C_GROUPS, C_HEADS // C_GROUPS, C_HEAD_DIM, C_D_STATE).transpose(0, 1, 4, 2, 3).reshape(
        b, C_GROUPS, C_D_STATE, C_GW)


def _state_from_t(ht):
    b = ht.shape[0]
    return ht.reshape(b, C_GROUPS, C_D_STATE, C_HEADS // C_GROUPS, C_HEAD_DIM).transpose(0, 1, 3, 4, 2).reshape(
        b, C_HEADS, C_HEAD_DIM, C_D_STATE)


def _router_kernel(x_ref, g_ref, w_ref, b_ref, base_ref, *rest, n_tiles):
    h_ref, mi_ref, mf_ref, cnt_ref = rest[-4:]
    i = pl.program_id(0)

    @pl.when(i == 0)
    def _():
        cnt_ref[...] = base_ref[...]

    @pl.when(i < n_tiles)
    def _():
        _route_tile(x_ref, g_ref, w_ref, b_ref, h_ref, mi_ref, mf_ref, cnt_ref)

    @pl.when(i >= n_tiles)
    def _():
        h_ref[...] = jnp.zeros_like(h_ref)


def _route_tile(x_ref, g_ref, w_ref, b_ref, h_ref, mi_ref, mf_ref, cnt_ref):
    tm = x_ref.shape[0]
    h = _rmsnorm_rows(x_ref[...], g_ref[...])
    h_ref[...] = h
    h_hi, h_lo, _ = _split3(h)
    w_hi, w_lo, _ = _split3(w_ref[...])
    logits = (_dot(h_lo, w_hi) + _dot(h_hi, w_lo)) + _dot(h_hi, w_hi) + b_ref[...]

    lane = lax.broadcasted_iota(jnp.int32, (tm, LANES), 1).astype(F32)
    big = float(4 * LANES)
    lg = jnp.where(lane < E_GROUPS, logits, -jnp.inf)
    mg = jnp.max(lg, axis=-1, keepdims=True)
    p_top = 1.0 / jnp.sum(jnp.exp(lg - mg), axis=-1, keepdims=True)
    g_top = jnp.min(jnp.where(lg == mg, lane, big), axis=-1, keepdims=True)
    lo = E_GROUPS + g_top * E_PER_GROUP
    le = jnp.where(jnp.logical_and(lane >= lo, lane < lo + E_PER_GROUP), logits, -jnp.inf)
    v1 = jnp.max(le, axis=-1, keepdims=True)
    i1 = jnp.min(jnp.where(le == v1, lane, big), axis=-1, keepdims=True)
    le2 = jnp.where(lane == i1, -jnp.inf, le)
    v2 = jnp.max(le2, axis=-1, keepdims=True)
    i2 = jnp.min(jnp.where(le2 == v2, lane, big), axis=-1, keepdims=True)
    e21 = jnp.exp(v2 - v1)
    gate1 = p_top / (1.0 + e21)
    gate2 = p_top * e21 / (1.0 + e21)
    eid1 = i1 - E_GROUPS
    eid2 = i2 - E_GROUPS

    oh1 = lane == eid1
    oh2 = lane == eid2
    oh = jnp.where(jnp.logical_or(oh1, oh2), 1.0, 0.0)
    ri = lax.broadcasted_iota(jnp.int32, (tm, tm), 0)
    ci = lax.broadcasted_iota(jnp.int32, (tm, tm), 1)
    before = jnp.where(ci < ri, 1.0, 0.0).astype(BF16)
    pos = _dot(before, oh.astype(BF16)) + cnt_ref[...]
    rank1 = jnp.sum(jnp.where(oh1, pos, 0.0), axis=-1, keepdims=True)
    rank2 = jnp.sum(jnp.where(oh2, pos, 0.0), axis=-1, keepdims=True)
    cnt_ref[...] = cnt_ref[...] + jnp.sum(oh, axis=0, keepdims=True)

    mi = jnp.where(lane == 0, eid1, jnp.where(lane == 1, eid2, jnp.where(lane == 2, rank1,
                                                                         jnp.where(lane == 3, rank2, 0.0))))
    sel = jnp.where(lax.broadcasted_iota(jnp.int32, (META_ROWS, LANES), 0)
                    == lax.broadcasted_iota(jnp.int32, (META_ROWS, LANES), 1), 1.0, 0.0).astype(BF16)
    mi_t = functools.reduce(jnp.add, [_dot_nt(sel, p) for p in _split3(mi)])
    mi_ref[...] = mi_t.astype(jnp.int32)
    mf_ref[...] = jnp.where(lane == 0, gate1, jnp.where(lane == 1, gate2, 0.0))


def _router(x, g, w_all, b_all, base, *, tm, h_into=None, h_row0=0):
    m = x.shape[0]
    n_tiles = m // tm
    last = n_tiles - 1
    tile = lambda i: jnp.minimum(i, last)
    one = pl.BlockSpec((1, LANES), lambda i: (0, 0))
    in_specs = [
        pl.BlockSpec((tm, D_MODEL), lambda i: (tile(i), 0)),
        pl.BlockSpec((1, D_MODEL), lambda i: (0, 0)),
        pl.BlockSpec((D_MODEL, LANES), lambda i: (0, 0)),
        one, one,
    ]
    args = [x, g.reshape(1, D_MODEL), w_all, b_all, base]
    if h_into is None:
        steps, h_rows, aliases = n_tiles + 1, (n_tiles + 1) * tm, {}
    else:
        steps, h_rows, aliases = n_tiles, h_into.shape[0], {len(args): 0}
        in_specs.append(pl.BlockSpec(memory_space=pl.ANY))
        args.append(h_into)
    return pl.pallas_call(
        functools.partial(_router_kernel, n_tiles=n_tiles),
        grid=(steps,),
        in_specs=in_specs,
        out_specs=[pl.BlockSpec((tm, D_MODEL), lambda i: (h_row0 // tm + i, 0)),
                   pl.BlockSpec((META_ROWS, tm), lambda i: (0, tile(i))),
                   pl.BlockSpec((tm, LANES), lambda i: (tile(i), 0)), one],
        out_shape=[
            jax.ShapeDtypeStruct((h_rows, D_MODEL), F32),
            jax.ShapeDtypeStruct((META_ROWS, m), jnp.int32),
            jax.ShapeDtypeStruct((m, LANES), F32),
            jax.ShapeDtypeStruct((1, LANES), F32),
        ],
        input_output_aliases=aliases,
        compiler_params=_cparams(("arbitrary",)),
        name="moe_router",
    )(*args)


def _gather_rows(idx_ref, base, n_rows, src_hbm, dst, sem):
    for r in range(n_rows):
        t = idx_ref[base + r]
        pltpu.make_async_copy(src_hbm.at[pl.ds(t, 1)], dst.at[pl.ds(r, 1)], sem).start()


def _wait_rows(n_rows, src_hbm, dst, sem):
    pltpu.make_async_copy(src_hbm.at[pl.ds(0, n_rows)], dst, sem).wait()


def _moe_ffn_kernel(be_ref, nu_ref, nv_ref, tok_ref, h_hbm, wg_ref, wu_ref, wd_ref, y_ref, xbuf, sem):
    i = pl.program_id(0)
    n_used = nu_ref[0]
    n_groups = MOE_ROWS // GATHER_ROWS

    def fetch(block):
        slot = block % 2
        for grp in range(n_groups):
            @pl.when(grp * GATHER_ROWS < nv_ref[block])
            def _():
                _gather_rows(tok_ref, block * MOE_ROWS + grp * GATHER_ROWS, GATHER_ROWS, h_hbm,
                             xbuf.at[slot, pl.ds(grp * GATHER_ROWS, GATHER_ROWS)], sem.at[slot, grp])

    @pl.when(i == 0)
    def _():
        xbuf[...] = jnp.zeros_like(xbuf)
        fetch(0)

    @pl.when(i + 1 < n_used)
    def _():
        fetch(i + 1)

    @pl.when(i < n_used)
    def _():
        slot = i % 2
        for grp in range(n_groups):
            @pl.when(grp * GATHER_ROWS < nv_ref[i])
            def _():
                _wait_rows(GATHER_ROWS, h_hbm, xbuf.at[slot, pl.ds(grp * GATHER_ROWS, GATHER_ROWS)],
                           sem.at[slot, grp])
        x = xbuf[slot].astype(BF16)
        gate = _dot(x, wg_ref[0, 0].astype(BF16))
        up = _dot(x, wu_ref[0, 0].astype(BF16))
        act = (gate * jax.nn.sigmoid(gate) * up).astype(BF16)
        y_ref[...] = _dot(act, wd_ref[0, 0].astype(BF16))

    @pl.when(i >= n_used)
    def _():
        y_ref[...] = jnp.zeros_like(y_ref)


def _moe_ffn(h_all, plan, w_gate, w_up, w_down, layer):
    blk_expert, n_used, n_valid, tok_of_slot = plan
    n_blocks = tok_of_slot.shape[0] // MOE_ROWS
    rows = lambda i, be, nu, nv, tok: (i, 0)
    wmap = lambda i, be, nu, nv, tok: (layer, be[i], 0, 0)
    return pl.pallas_call(
        _moe_ffn_kernel,
        grid_spec=pltpu.PrefetchScalarGridSpec(
            num_scalar_prefetch=4,
            grid=(n_blocks,),
            in_specs=[
                pl.BlockSpec(memory_space=pl.ANY),
                pl.BlockSpec((1, 1, D_MODEL, E_FF), wmap),
                pl.BlockSpec((1, 1, D_MODEL, E_FF), wmap),
                pl.BlockSpec((1, 1, E_FF, D_MODEL), wmap),
            ],
            out_specs=pl.BlockSpec((MOE_ROWS, D_MODEL), rows),
            scratch_shapes=[pltpu.VMEM((2, MOE_ROWS, D_MODEL), F32),
                            pltpu.SemaphoreType.DMA((2, MOE_ROWS // GATHER_ROWS))],
        ),
        out_shape=jax.ShapeDtypeStruct((n_blocks * MOE_ROWS, D_MODEL), F32),
        compiler_params=_cparams(("arbitrary",), unchecked_dma=True),
        name="moe_ffn",
    )(blk_expert, n_used, n_valid, tok_of_slot, h_all, w_gate, w_up, w_down)


def _moe_plan_kernel(meta_ref, cnt_ref, be_ref, nu_ref, nv_ref, tok_ref, dp_ref, ds_ref, start_scr,
                     *, n_p, n_s, n_blocks):
    n_tok = n_p + n_s

    def segment(e, first_blk):
        start_scr[e] = first_blk * MOE_ROWS
        cnt = cnt_ref[e]
        nb = (cnt + MOE_ROWS - 1) // MOE_ROWS

        def fill(j, carry):
            be_ref[first_blk + j] = e
            nv_ref[first_blk + j] = jnp.minimum(cnt - j * MOE_ROWS, MOE_ROWS)
            return carry
        lax.fori_loop(0, nb, fill, 0)
        return first_blk + nb
    n_used = lax.fori_loop(0, N_EXPERTS, segment, 0)
    nu_ref[0] = n_used

    last_expert = be_ref[jnp.maximum(n_used - 1, 0)]

    def tail(j, carry):
        be_ref[j] = last_expert
        nv_ref[j] = 0
        return carry
    lax.fori_loop(n_used, n_blocks, tail, 0)

    def clear(j, carry):
        tok_ref[j] = 0
        return carry
    lax.fori_loop(0, n_blocks * MOE_ROWS, clear, 0, unroll=8)

    def place(dest_ref, first, count):
        def body(j, carry):
            t = first + j
            for k in range(2):
                d = start_scr[meta_ref[k * n_tok + t]] + meta_ref[(2 + k) * n_tok + t]
                tok_ref[d] = t
                dest_ref[k * count + j] = d
            return carry
        lax.fori_loop(0, count, body, 0, unroll=4)
    place(dp_ref, 0, n_p)
    place(ds_ref, n_p, n_s)


def _moe_plan(meta, counts, *, n_p, n_s):
    n_blocks = (2 * (n_p + n_s)) // MOE_ROWS + N_EXPERTS
    smem = pl.BlockSpec(memory_space=pltpu.SMEM)
    i32 = lambda n: jax.ShapeDtypeStruct((n,), jnp.int32)
    be, nu, nv, tok, dp, ds = pl.pallas_call(
        functools.partial(_moe_plan_kernel, n_p=n_p, n_s=n_s, n_blocks=n_blocks),
        in_specs=[smem, smem],
        out_specs=[smem] * 6,
        out_shape=[i32(n_blocks), i32(1), i32(n_blocks), i32(n_blocks * MOE_ROWS), i32(2 * n_p), i32(2 * n_s)],
        scratch_shapes=[pltpu.SMEM((N_EXPERTS,), jnp.int32)],
        name="moe_plan",
    )(meta, counts)
    return (be, nu, nv, tok), dp, ds


def _moe_combine_kernel(dest_ref, x_ref, g_ref, y_hbm, o_ref, ybuf, sem, *, tm, n_tiles):
    i = pl.program_id(0)

    def fetch(tile):
        slot = tile % 2
        for k in range(2):
            _gather_rows(dest_ref, (k * n_tiles + tile) * tm, tm, y_hbm, ybuf.at[slot, k], sem.at[slot, k])

    @pl.when(i == 0)
    def _():
        fetch(0)

    @pl.when(i + 1 < n_tiles)
    def _():
        fetch(i + 1)

    slot = i % 2
    acc = x_ref[...]
    for k in range(2):
        _wait_rows(tm, y_hbm, ybuf.at[slot, k], sem.at[slot, k])
        acc = acc + g_ref[:, k:k + 1] * ybuf[slot, k]
    o_ref[...] = acc


def _moe_combine(x, gates, dest, y_rows, *, tm):
    m = x.shape[0]
    n_tiles = m // tm
    return pl.pallas_call(
        functools.partial(_moe_combine_kernel, tm=tm, n_tiles=n_tiles),
        grid_spec=pltpu.PrefetchScalarGridSpec(
            num_scalar_prefetch=1,
            grid=(n_tiles,),
            in_specs=[
                pl.BlockSpec((tm, D_MODEL), lambda i, d: (i, 0)),
                pl.BlockSpec((tm, LANES), lambda i, d: (i, 0)),
                pl.BlockSpec(memory_space=pl.ANY),
            ],
            out_specs=pl.BlockSpec((tm, D_MODEL), lambda i, d: (i, 0)),
            scratch_shapes=[pltpu.VMEM((2, 2, tm, D_MODEL), F32), pltpu.SemaphoreType.DMA((2, 2))],
        ),
        out_shape=jax.ShapeDtypeStruct((m, D_MODEL), F32),
        compiler_params=_cparams(("arbitrary",), unchecked_dma=True),
        name="moe_combine",
    )(dest, x, gates, y_rows)


def _moe(xp, xs, g, w_rg, b_rg, w_re, b_re, w_gate, w_up, w_down, layer):
    n_p, n_s = xp.shape[0], xs.shape[0]
    pad = LANES - E_GROUPS - N_EXPERTS
    w_all = jnp.pad(jnp.concatenate([w_rg, w_re.reshape(D_MODEL, N_EXPERTS)], axis=1), ((0, 0), (0, pad)))
    b_all = jnp.pad(jnp.concatenate([b_rg, b_re.reshape(N_EXPERTS)]), (0, pad)).reshape(1, LANES)
    h_all, mip, mfp, cnt_p = _router(xp, g, w_all, b_all, jnp.zeros((1, LANES), F32), tm=512)
    h_all, mis, mfs, cnt = _router(xs, g, w_all, b_all, cnt_p, tm=n_s, h_into=h_all, h_row0=n_p)
    meta = jnp.concatenate([mip[:4], mis[:4]], axis=1).reshape(-1)
    plan, dest_p, dest_s = _moe_plan(meta, cnt[0].astype(jnp.int32), n_p=n_p, n_s=n_s)
    y_rows = _moe_ffn(h_all, plan, w_gate, w_up, w_down, layer)
    return (_moe_combine(xp, mfp, dest_p, y_rows, tm=COMBINE_ROWS),
            _moe_combine(xs, mfs, dest_s, y_rows, tm=n_s))


def _kv_shift_kernel(*refs, n_groups):
    caches, news, outs = refs[:n_groups], refs[n_groups:2 * n_groups], refs[2 * n_groups:3 * n_groups]
    sem = refs[-1]
    copies = []
    for c, new, out in zip(caches, news, outs):
        layers, b, w = c.shape[:3]
        ts = new.shape[2]
        for l in range(layers):
            for i in range(b):
                copies.append(pltpu.make_async_copy(c.at[l, i, pl.ds(ts, w - ts)],
                                                    out.at[l, i, pl.ds(0, w - ts)], sem.at[0]))
                copies.append(pltpu.make_async_copy(new.at[l, i], out.at[l, i, pl.ds(w - ts, ts)], sem.at[0]))
    for cp in copies:
        cp.start()
    for cp in copies:
        cp.wait()


def _kv_shift(caches, news):
    n = len(caches)
    hbm = pl.BlockSpec(memory_space=pl.ANY)
    return pl.pallas_call(
        functools.partial(_kv_shift_kernel, n_groups=n),
        in_specs=[hbm] * (2 * n),
        out_specs=[hbm] * n,
        out_shape=[jax.ShapeDtypeStruct(c.shape, c.dtype) for c in caches],
        scratch_shapes=[pltpu.SemaphoreType.DMA((1,))],
        name="kv_shift",
    )(*caches, *news)


def _final_norm_kernel(x_ref, g_ref, o_ref):
    o_ref[...] = _rmsnorm_rows(x_ref[...], g_ref[...])


def _final_norm(x, g, *, tm):
    m = x.shape[0]
    return pl.pallas_call(
        _final_norm_kernel,
        grid=(m // tm,),
        in_specs=[pl.BlockSpec((tm, D_MODEL), lambda i: (i, 0)), pl.BlockSpec((1, D_MODEL), lambda i: (0, 0))],
        out_specs=pl.BlockSpec((tm, D_MODEL), lambda i: (i, 0)),
        out_shape=jax.ShapeDtypeStruct((m, D_MODEL), F32),
        compiler_params=_cparams(("parallel",)),
        name="final_norm",
    )(x, g.reshape(1, D_MODEL))


def _rope_tables(pos):
    half = HEAD_DIM // 2
    inv = 1.0 / (ROPE_THETA ** (jnp.arange(half, dtype=F32) / half))
    ang = pos.astype(F32)[:, None] * inv[None, :]
    cos, sin = jnp.cos(ang), jnp.sin(ang)
    return jnp.concatenate([cos, cos], -1), jnp.concatenate([-sin, sin], -1)


def _kv_state(proj3, group, keep):
    b, t, _ = proj3.shape
    base = 2 * A_WIDTH + group * 3 * B_HW
    k = proj3[:, t - keep:, base + B_HW: base + 2 * B_HW]
    v = proj3[:, t - keep:, base + 2 * B_HW: base + 3 * B_HW]
    return jnp.stack([k, v], axis=2).reshape(b, keep, 2, B_HEADS, HEAD_DIM)


def _layer_ab(xp, xs, g, w_in_b, ln_g, ln_b, w_s, b_s, w_out_b, cache_views, layer, tabs_p, tabs_s, shapes):
    bp, tp, bs, ts = shapes

    proj_p = _inproj(xp, g, w_in_b, layer, tabs_p, tm=1024)
    a_p = _gmlp(proj_p, ln_g, ln_b, w_s, b_s, n_rows=bp * tp, chunks=2, emit_v=False)[0]
    outs, lses = zip(*[_attn_prompt(proj_p, grp, batch=bp, seq=tp) for grp in range(len(B_CONFIGS))])
    b_p = _merge(outs, lses, tm=1024)
    xp_new = _outproj([a_p, b_p], w_out_b, layer, xp, tm=1024, tn=1024)
    proj_p3 = proj_p.reshape(bp, tp, AB_IN)
    kv_p = [_kv_state(proj_p3, grp, min(w, tp)) for grp, (w, _) in enumerate(B_CONFIGS)]

    n_s = bs * ts
    proj_s = _inproj(xs, g, w_in_b, layer, tabs_s, tm=n_s)
    proj_s3 = proj_s.reshape(bs, ts, AB_IN)
    chunk_in = jnp.pad(proj_s3[:, :, :2 * A_WIDTH], ((0, 0), (0, A_CHUNK - ts), (0, 0)))
    a_s, v_s = _gmlp(chunk_in.reshape(bs * A_CHUNK, 2 * A_WIDTH), ln_g, ln_b, w_s, b_s,
                     n_rows=bs * A_CHUNK, chunks=1, emit_v=True)
    a_s = a_s.reshape(bs, A_CHUNK, A_WIDTH)[:, :ts].reshape(n_s, A_WIDTH)
    v_s = v_s.reshape(bs, A_CHUNK, A_WIDTH)[:, :ts]
    qkv_s = jnp.pad(proj_s3[:, :, 2 * A_WIDTH:], ((0, 0), (0, SAMPLE_ROWS - ts), (0, 0)))
    b_s_out = _attn_sample(qkv_s, cache_views, layer)[:, :ts].reshape(n_s, B_HW)
    xs_new = _outproj([a_s, b_s_out], w_out_b, layer, xs, tm=n_s, tn=1024)
    kv_s = [_kv_state(proj_s3, grp, ts) for grp in range(len(B_CONFIGS))]
    return xp_new, xs_new, kv_p, kv_s, v_s


def _layer_c(xp, xs, g, w_main, w_dt, conv_w, conv_b, dt_bias, a_log, d_skip, norm_g, w_out_b, layer,
             conv_state, ssm_state, shapes):
    bp, tp, bs, ts = shapes
    n_main = C_D_INNER + C_CONV_DIM
    args = (conv_w, conv_b, dt_bias, a_log, d_skip, norm_g)

    proj_p = _inproj(xp, g, w_main, layer, tm=1024, n_out=n_main)
    dt_p = _inproj(xp, g, w_dt, layer, tm=1024)
    zc = jnp.zeros((bp, CONV_PAD, C_CONV_DIM), F32)
    zh = jnp.zeros((bp, C_GROUPS, C_D_STATE, C_GW), F32)
    q_p = 128
    y_p, h_p = _ssd(proj_p, dt_p, zc, zh, *args, batch=bp, n_chunks=tp // q_p, q_len=q_p, t_valid=q_p)
    xp_new = _outproj([y_p], w_out_b, layer, xp, tm=512, tn=1024)
    conv_p = proj_p.reshape(bp, tp, n_main)[:, tp - (C_CONV - 1):, C_D_INNER:]

    n_s = bs * ts
    q_s = 16
    proj_s = _inproj(xs, g, w_main, layer, tm=n_s, n_out=n_main)
    dt_s = _inproj(xs, g, w_dt, layer, tm=n_s)
    pad_rows = lambda a: jnp.pad(a.reshape(bs, ts, -1), ((0, 0), (0, q_s - ts), (0, 0))).reshape(bs * q_s, -1)
    cst = jnp.pad(conv_state, ((0, 0), (CONV_PAD - (C_CONV - 1), 0), (0, 0)))
    y_s, h_s = _ssd(pad_rows(proj_s), pad_rows(dt_s), cst, _state_to_t(ssm_state), *args,
                    batch=bs, n_chunks=1, q_len=q_s, t_valid=ts)
    y_s = y_s.reshape(bs, q_s, C_D_INNER)[:, :ts].reshape(n_s, C_D_INNER)
    xs_new = _outproj([y_s], w_out_b, layer, xs, tm=n_s, tn=1024)
    assert ts >= C_CONV - 1
    conv_s = proj_s.reshape(bs, ts, n_main)[:, ts - (C_CONV - 1):, C_D_INNER:]
    return xp_new, xs_new, conv_p, conv_s, _state_from_t(h_p), _state_from_t(h_s)


def kernel(x_prompt, x_sample, cache_kv_w128, cache_kv_w512, cache_kv_w2048, state_conv, state_ssm,
           norm_mix, norm_ffn, norm_final, w_in_ab, a_ln_g, a_ln_b, a_w_s, a_b_s, w_out_ab,
           w_in_c, c_conv_w, c_conv_b, c_dt_bias, c_a_log, c_d, c_norm_g, w_out_c,
           w_router_g, b_router_g, w_router_e, b_router_e, w_exp_gate, w_exp_up, w_exp_down):
    bp, tp, _ = x_prompt.shape
    bs, ts, _ = x_sample.shape
    shapes = (bp, tp, bs, ts)
    depth = norm_mix.shape[0]
    kv_caches = (cache_kv_w128, cache_kv_w512, cache_kv_w2048)
    tabs_p = _rope_tables(jnp.arange(tp, dtype=jnp.int32))
    pos_s = PAST_LEN + jnp.arange(ts, dtype=jnp.int32)
    tabs_s = tuple(jnp.tile(t, (bs, 1)) for t in _rope_tables(pos_s))

    w_in_ab_b = w_in_ab.astype(BF16)
    w_out_ab_b = w_out_ab.astype(BF16)
    w_in_c_b = w_in_c.astype(BF16)
    w_dt_b = jnp.pad(w_in_c[:, :, C_D_INNER + C_CONV_DIM:], ((0, 0), (0, 0), (0, LANES - C_HEADS))).astype(BF16)
    w_out_c_b = w_out_c.astype(BF16)
    cache_views = _cache_views(kv_caches, ts)

    xp = x_prompt.reshape(bp * tp, D_MODEL)
    xs = x_sample.reshape(bs * ts, D_MODEL)
    kv_p = [[] for _ in B_CONFIGS]
    kv_s = [[] for _ in B_CONFIGS]
    chunk_v, conv_p, conv_s, ssm_p, ssm_s = [], [], [], [], []
    for l in range(depth):
        i = l // 2
        if l % 2 == 0:
            xp, xs, nkv_p, nkv_s, v_s = _layer_ab(
                xp, xs, norm_mix[l], w_in_ab_b, a_ln_g[i], a_ln_b[i], a_w_s[i], a_b_s[i], w_out_ab_b,
                cache_views, i, tabs_p, tabs_s, shapes)
            for grp in range(len(B_CONFIGS)):
                kv_p[grp].append(nkv_p[grp])
                kv_s[grp].append(nkv_s[grp])
            chunk_v.append(v_s)
        else:
            xp, xs, ncp, ncs, nsp, nss = _layer_c(
                xp, xs, norm_mix[l], w_in_c_b, w_dt_b, c_conv_w[i], c_conv_b[i], c_dt_bias[i], c_a_log[i], c_d[i],
                c_norm_g[i], w_out_c_b, i, state_conv[i], state_ssm[i], shapes)
            conv_p.append(ncp)
            conv_s.append(ncs)
            ssm_p.append(nsp)
            ssm_s.append(nss)
        xp, xs = _moe(xp, xs, norm_ffn[l], w_router_g[l], b_router_g[l], w_router_e[l], b_router_e[l],
                      w_exp_gate, w_exp_up, w_exp_down, l)
    y_p = _final_norm(xp, norm_final, tm=1024).reshape(bp, tp, D_MODEL)
    y_s = _final_norm(xs, norm_final, tm=bs * ts).reshape(bs, ts, D_MODEL)
    kv_s = _kv_shift(kv_caches, [jnp.stack(new) for new in kv_s])
    return (y_p, y_s,
            jnp.stack(kv_p[0]), kv_s[0],
            jnp.stack(kv_p[1]), kv_s[1],
            jnp.stack(kv_p[2]), kv_s[2],
            jnp.stack(chunk_v),
            jnp.stack(conv_p), jnp.stack(conv_s),
            jnp.stack(ssm_p), jnp.stack(ssm_s))
```

```python
import functools
import math

import jax
import jax.numpy as jnp
from jax import lax
from jax.experimental import pallas as pl
from jax.experimental.pallas import tpu as pltpu

F32 = jnp.float32
BF16 = jnp.bfloat16

D_MODEL = 2048
PAST_LEN = 16384
EPS = 1e-6
A_WIDTH = 1024
A_GROUPS = 8
A_GW = 128
A_CHUNK = 128
HEAD_DIM = 128
B_HEADS = 4
B_CONFIGS = ((128, 1), (512, 4), (2048, 16))
B_BAND = 128
B_HW = B_HEADS * HEAD_DIM
AB_IN = 2 * A_WIDTH + 3 * len(B_CONFIGS) * B_HW
ROPE_THETA = 10000.0
C_D_INNER = 4096
C_HEAD_DIM = 64
C_HEADS = 64
C_GROUPS = 8
C_D_STATE = 128
C_CONV = 4
C_CONV_DIM = C_D_INNER + 2 * C_GROUPS * C_D_STATE
C_GW = C_D_INNER // C_GROUPS
E_GROUPS = 8
E_PER_GROUP = 8
N_EXPERTS = 64
E_FF = 512
MOE_ROWS = 256
GATHER_ROWS = 32
COMBINE_ROWS = 128
META_ROWS = 8

LANES = 128
VMEM_LIMIT = 56 * 1024 * 1024


def _cparams(sem, unchecked_dma=False):
    return pltpu.CompilerParams(dimension_semantics=sem, vmem_limit_bytes=VMEM_LIMIT,
                                disable_bounds_checks=unchecked_dma)


def _split3(x):
    hi = x.astype(BF16)
    r1 = x - hi.astype(F32)
    mid = r1.astype(BF16)
    lo = (r1 - mid.astype(F32)).astype(BF16)
    return hi, mid, lo


def _dot(a, b):
    return jnp.dot(a, b, preferred_element_type=F32)


def _dot_nt(a, b):
    return lax.dot_general(a, b, (((1,), (1,)), ((), ())), preferred_element_type=F32)


def _rmsnorm_rows(x, g):
    ms = jnp.mean(x * x, axis=-1, keepdims=True)
    return x * lax.rsqrt(ms + EPS) * g


def _inproj_kernel(x_ref, g_ref, w_ref, *rest, rope):
    o_ref, h_scr = rest[-2:]
    j = pl.program_id(1)

    @pl.when(j == 0)
    def _():
        h_scr[...] = _rmsnorm_rows(x_ref[...], g_ref[...]).astype(BF16)

    acc = _dot(h_scr[...], w_ref[0])
    if not rope:
        o_ref[...] = acc
        return
    cos_ref, sin_ref = rest[:2]
    is_qk = jnp.logical_and(j >= 4, (j - 4) % 3 != 2)

    @pl.when(is_qk)
    def _():
        cos = cos_ref[...]
        sin = sin_ref[...]
        for h in range(B_HEADS):
            sl = slice(h * HEAD_DIM, (h + 1) * HEAD_DIM)
            a = acc[:, sl]
            o_ref[:, sl] = a * cos + pltpu.roll(a, HEAD_DIM // 2, 1) * sin

    @pl.when(jnp.logical_not(is_qk))
    def _():
        o_ref[...] = acc


def _inproj(x, g, w_bf16, layer, tabs=None, *, tm, n_out=None):
    m, _ = x.shape
    n = n_out or w_bf16.shape[2]
    tn = B_HW if n % B_HW == 0 else n
    in_specs = [
        pl.BlockSpec((tm, D_MODEL), lambda i, j: (i, 0)),
        pl.BlockSpec((1, D_MODEL), lambda i, j: (0, 0)),
        pl.BlockSpec((1, D_MODEL, tn), lambda i, j: (layer, 0, j)),
    ]
    args = [x, g.reshape(1, D_MODEL), w_bf16]
    if tabs is not None:
        n_pos_tiles = tabs[0].shape[0] // tm
        in_specs += [pl.BlockSpec((tm, HEAD_DIM), lambda i, j: (i % n_pos_tiles, 0))] * 2
        args += list(tabs)
    return pl.pallas_call(
        functools.partial(_inproj_kernel, rope=tabs is not None),
        grid=(m // tm, n // tn),
        in_specs=in_specs,
        out_specs=pl.BlockSpec((tm, tn), lambda i, j: (i, j)),
        out_shape=jax.ShapeDtypeStruct((m, n), F32),
        scratch_shapes=[pltpu.VMEM((tm, D_MODEL), BF16)],
        compiler_params=_cparams(("parallel", "arbitrary")),
        name="inproj",
    )(*args)


def _gmlp_kernel(p_ref, lng_ref, lnb_ref, ws_ref, bs_ref, a_ref, *v_ref, chunks):
    row = lax.broadcasted_iota(jnp.int32, (A_CHUNK, A_CHUNK), 0)
    col = lax.broadcasted_iota(jnp.int32, (A_CHUNK, A_CHUNK), 1)
    causal = col <= row
    for c in range(chunks):
        rows = slice(c * A_CHUNK, (c + 1) * A_CHUNK)
        u = jax.nn.gelu(p_ref[rows, 0:A_WIDTH], approximate=True)
        vg = jax.nn.gelu(p_ref[rows, A_WIDTH:2 * A_WIDTH], approximate=True)
        vc = vg - jnp.mean(vg, axis=-1, keepdims=True)
        v = vc * lax.rsqrt(jnp.mean(vc * vc, axis=-1, keepdims=True) + EPS)
        v = v * lng_ref[...] + lnb_ref[...]
        if v_ref:
            v_ref[0][rows, :] = v
        vb = v.astype(BF16)
        for g in range(A_GROUPS):
            sl = slice(g * A_GW, (g + 1) * A_GW)
            w = jnp.where(causal, ws_ref[g], 0.0).astype(BF16)
            mixed = _dot(w, vb[:, sl]) + bs_ref[:, sl]
            a_ref[rows, sl] = (u[:, sl] * mixed).astype(BF16)


def _gmlp(proj, ln_g, ln_b, w_s, b_s, *, n_rows, chunks, emit_v):
    rows = chunks * A_CHUNK
    bsb = jnp.repeat(jnp.transpose(b_s), A_GW, axis=1)
    out_shape = [jax.ShapeDtypeStruct((n_rows, A_WIDTH), BF16)]
    out_specs = [pl.BlockSpec((rows, A_WIDTH), lambda i: (i, 0))]
    if emit_v:
        out_shape.append(jax.ShapeDtypeStruct((n_rows, A_WIDTH), F32))
        out_specs.append(pl.BlockSpec((rows, A_WIDTH), lambda i: (i, 0)))
    return pl.pallas_call(
        functools.partial(_gmlp_kernel, chunks=chunks),
        grid=(n_rows // rows,),
        in_specs=[
            pl.BlockSpec((rows, 2 * A_WIDTH), lambda i: (i, 0)),
            pl.BlockSpec((1, A_WIDTH), lambda i: (0, 0)),
            pl.BlockSpec((1, A_WIDTH), lambda i: (0, 0)),
            pl.BlockSpec((A_GROUPS, A_CHUNK, A_CHUNK), lambda i: (0, 0, 0)),
            pl.BlockSpec((A_CHUNK, A_WIDTH), lambda i: (0, 0)),
        ],
        out_specs=out_specs,
        out_shape=out_shape,
        compiler_params=_cparams(("parallel",)),
        name="gmlp",
    )(proj, ln_g.reshape(1, A_WIDTH), ln_b.reshape(1, A_WIDTH), w_s, bsb)


def _attn_prompt_kernel(q_ref, kp_ref, kc_ref, vp_ref, vc_ref, o_ref, lse_ref, *, dil, heads):
    n = pl.program_id(2)
    qi = lax.broadcasted_iota(jnp.int32, (B_BAND, 2 * B_BAND), 0)
    kc_idx = lax.broadcasted_iota(jnp.int32, (B_BAND, 2 * B_BAND), 1)
    dist = qi + B_BAND - kc_idx
    ok = jnp.logical_and(jnp.logical_and(dist >= 0, dist <= B_BAND),
                         jnp.logical_or(kc_idx >= B_BAND, n > 0))
    scale = HEAD_DIM ** -0.5
    for r in range(dil):
        rows = pl.ds(r, B_BAND, stride=dil) if dil > 1 else slice(None)
        for h in range(heads):
            sl = slice(h * HEAD_DIM, (h + 1) * HEAD_DIM)
            q = q_ref[rows, sl].astype(BF16)
            k = jnp.concatenate([kp_ref[rows, sl], kc_ref[rows, sl]], axis=0).astype(BF16)
            v = jnp.concatenate([vp_ref[rows, sl], vc_ref[rows, sl]], axis=0).astype(BF16)
            s = jnp.where(ok, _dot_nt(q, k) * scale, -jnp.inf)
            m = jnp.max(s, axis=-1, keepdims=True)
            p = jnp.exp(s - m)
            l = jnp.sum(p, axis=-1, keepdims=True)
            o_ref[rows, sl] = _dot(p.astype(BF16), v) / l
            lse_ref[rows, sl] = jnp.broadcast_to(m + jnp.log(l), (B_BAND, HEAD_DIM))


def _attn_prompt(proj, group, *, batch, seq):
    dil = B_CONFIGS[group][1]
    rows = B_BAND * dil
    nblk = seq // rows
    heads = B_HEADS if dil == 1 else 1
    width = heads * HEAD_DIM
    n_col = B_HW // width
    qcol = (4 + 3 * group) * n_col

    def spec(off, prev):
        def imap(b, h, n):
            nn = jnp.maximum(n - 1, 0) if prev else n
            return (b * nblk + nn, qcol + off * n_col + h)
        return pl.BlockSpec((rows, width), imap)

    out_spec = pl.BlockSpec((rows, width), lambda b, h, n: (b * nblk + n, h))
    return pl.pallas_call(
        functools.partial(_attn_prompt_kernel, dil=dil, heads=heads),
        grid=(batch, n_col, nblk),
        in_specs=[spec(0, False), spec(1, True), spec(1, False), spec(2, True), spec(2, False)],
        out_specs=[out_spec, out_spec],
        out_shape=[jax.ShapeDtypeStruct((batch * seq, B_HW), F32)] * 2,
        compiler_params=_cparams(("parallel", "parallel", "arbitrary")),
        name=f"attn_prompt_g{group}",
    )(proj, proj, proj, proj, proj)


def _merge_kernel(o0, o1, o2, l0, l1, l2, b_ref):
    m = jnp.maximum(jnp.maximum(l0[...], l1[...]), l2[...])
    e0 = jnp.exp(l0[...] - m)
    e1 = jnp.exp(l1[...] - m)
    e2 = jnp.exp(l2[...] - m)
    den = e0 + e1 + e2
    b_ref[...] = ((e0 / den) * o0[...] + (e1 / den) * o1[...] + (e2 / den) * o2[...]).astype(BF16)


def _merge(outs, lses, *, tm):
    m = outs[0].shape[0]
    spec = pl.BlockSpec((tm, B_HW), lambda i: (i, 0))
    return pl.pallas_call(
        _merge_kernel,
        grid=(m // tm,),
        in_specs=[spec] * 6,
        out_specs=spec,
        out_shape=jax.ShapeDtypeStruct((m, B_HW), BF16),
        compiler_params=_cparams(("parallel",)),
        name="attn_merge",
    )(*outs, *lses)


SAMPLE_ROWS = 16


def _attn_sample_kernel(qkv_ref, c0_ref, c1_ref, c2_ref, b_ref, *, n_real):
    caches = (c0_ref, c1_ref, c2_ref)
    row_w = 2 * B_HW
    scale = HEAD_DIM ** -0.5
    qi_c = lax.broadcasted_iota(jnp.int32, (SAMPLE_ROWS, B_BAND), 0)
    ka_c = lax.broadcasted_iota(jnp.int32, (SAMPLE_ROWS, B_BAND), 1)
    qi_n = lax.broadcasted_iota(jnp.int32, (SAMPLE_ROWS, SAMPLE_ROWS), 0)
    km_n = lax.broadcasted_iota(jnp.int32, (SAMPLE_ROWS, SAMPLE_ROWS), 1)

    def valid(diff, dil):
        shift = dil.bit_length() - 1
        ok = jnp.logical_and(diff >= 0, (diff & (dil - 1)) == 0)
        return jnp.logical_and(ok, (diff >> shift) <= B_BAND)

    for h in range(B_HEADS):
        hs = slice(h * HEAD_DIM, (h + 1) * HEAD_DIM)
        outs, lses = [], []
        for g, (window, dil) in enumerate(B_CONFIGS):
            base = g * 3 * B_HW
            q = qkv_ref[0, :, base + h * HEAD_DIM: base + (h + 1) * HEAD_DIM].astype(BF16)
            kn = qkv_ref[0, :, base + B_HW + h * HEAD_DIM: base + B_HW + (h + 1) * HEAD_DIM].astype(BF16)
            vn = qkv_ref[0, :, base + 2 * B_HW + h * HEAD_DIM: base + 2 * B_HW + (h + 1) * HEAD_DIM].astype(BF16)
            n_res = min(dil, n_real)
            scores, values = [], []
            for r in range(n_res):
                kc = caches[g][0, 0, :, r * row_w + h * HEAD_DIM: r * row_w + (h + 1) * HEAD_DIM].astype(BF16)
                vc = caches[g][0, 0, :, r * row_w + B_HW + h * HEAD_DIM: r * row_w + B_HW + (h + 1) * HEAD_DIM].astype(BF16)
                s = _dot_nt(q, kc) * scale
                diff = qi_c + window - ka_c * dil - r
                scores.append(jnp.where(valid(diff, dil), s, -jnp.inf))
                values.append(vc)
            s = _dot_nt(q, kn) * scale
            scores.append(jnp.where(valid(qi_n - km_n, dil), s, -jnp.inf))
            values.append(vn)
            m = functools.reduce(jnp.maximum, [jnp.max(s, axis=-1, keepdims=True) for s in scores])
            ps = [jnp.exp(s - m) for s in scores]
            l = functools.reduce(jnp.add, [jnp.sum(p, axis=-1, keepdims=True) for p in ps])
            o = functools.reduce(jnp.add, [_dot(p.astype(BF16), v) for p, v in zip(ps, values)])
            outs.append(o / l)
            lses.append(m + jnp.log(l))
        m = functools.reduce(jnp.maximum, lses)
        es = [jnp.exp(l - m) for l in lses]
        den = functools.reduce(jnp.add, es)
        b_ref[0, :, hs] = functools.reduce(jnp.add, [(e / den) * o for e, o in zip(es, outs)]).astype(BF16)


def _cache_views(caches, n_real):
    views = []
    for (window, dil), c in zip(B_CONFIGS, caches):
        layers, b = c.shape[:2]
        assert c.shape[2] == window and window == dil * B_BAND
        n_res = min(dil, n_real)
        c = c.reshape(layers, b, B_BAND, dil, 2, B_HEADS, HEAD_DIM)[:, :, :, :n_res]
        views.append(c.reshape(layers, b, B_BAND, n_res * 2 * B_HW))
    return views


def _attn_sample(qkv, views, layer):
    b = qkv.shape[0]
    n_real = max(v.shape[3] // (2 * B_HW) for v in views)
    specs = [pl.BlockSpec((1, 1) + v.shape[2:], lambda i: (layer, i, 0, 0)) for v in views]
    return pl.pallas_call(
        functools.partial(_attn_sample_kernel, n_real=n_real),
        grid=(b,),
        in_specs=[pl.BlockSpec((1, SAMPLE_ROWS, qkv.shape[2]), lambda i: (i, 0, 0))] + specs,
        out_specs=pl.BlockSpec((1, SAMPLE_ROWS, B_HW), lambda i: (i, 0, 0)),
        out_shape=jax.ShapeDtypeStruct((b, SAMPLE_ROWS, B_HW), BF16),
        compiler_params=_cparams(("parallel",)),
        name="attn_sample",
    )(qkv, *views)


def _outproj_kernel(*refs, n_parts):
    a_refs = refs[:n_parts]
    w_refs = refs[n_parts:2 * n_parts]
    x_ref, o_ref = refs[2 * n_parts], refs[2 * n_parts + 1]
    acc = x_ref[...]
    for a, w in zip(a_refs, w_refs):
        acc = acc + _dot(a[...], w[0])
    o_ref[...] = acc


def _outproj(parts, w_bf16, layer, x, *, tm, tn):
    m, n = x.shape
    in_specs = [pl.BlockSpec((tm, a.shape[1]), lambda j, i: (i, 0)) for a in parts]
    row0 = 0
    for a in parts:
        width = a.shape[1]
        blk = row0 // width
        assert blk * width == row0
        in_specs.append(pl.BlockSpec((1, width, tn), lambda j, i, blk=blk: (layer, blk, j)))
        row0 += width
    assert row0 == w_bf16.shape[1]
    weights = [w_bf16] * len(parts)
    in_specs.append(pl.BlockSpec((tm, tn), lambda j, i: (i, j)))
    return pl.pallas_call(
        functools.partial(_outproj_kernel, n_parts=len(parts)),
        grid=(n // tn, m // tm),
        in_specs=in_specs,
        out_specs=pl.BlockSpec((tm, tn), lambda j, i: (i, j)),
        out_shape=jax.ShapeDtypeStruct((m, n), F32),
        compiler_params=_cparams(("parallel", "parallel")),
        name="outproj",
    )(*parts, *weights, x)


CONV_PAD = 8


def _ssd_kernel(z_ref, xa_ref, xb_ref, bc_ref, dtr_ref, dtrt_ref, cst_ref, cw_ref, cb_ref,
                dtb_ref, dtbc_ref, alog_ref, alogc_ref, dsk_ref, ng_ref, e_ref, h0_ref,
                y_ref, h_ref, ext_scr, act_scr, *, q_len, t_valid):
    c = pl.program_id(1)

    @pl.when(c == 0)
    def _():
        ext_scr[0:CONV_PAD, :] = cst_ref[0]
        h_ref[...] = h0_ref[...]

    half = C_CONV_DIM // 3
    ext_scr[CONV_PAD:CONV_PAD + q_len, 0:half] = xa_ref[...]
    ext_scr[CONV_PAD:CONV_PAD + q_len, half:2 * half] = xb_ref[...]
    ext_scr[CONV_PAD:CONV_PAD + q_len, 2 * half:3 * half] = bc_ref[...]
    for s in range(C_CONV_DIM // C_GW):
        sl = slice(s * C_GW, (s + 1) * C_GW)
        acc = cb_ref[:, sl]
        for j in range(C_CONV):
            lo = CONV_PAD - (C_CONV - 1) + j
            acc = acc + ext_scr[lo:lo + q_len, sl] * cw_ref[j:j + 1, sl]
        act_scr[:, sl] = acc * jax.nn.sigmoid(acc)
    ext_scr[0:CONV_PAD, :] = ext_scr[q_len:q_len + CONV_PAD, :]

    ti = lax.broadcasted_iota(jnp.int32, (q_len, q_len), 0)
    si = lax.broadcasted_iota(jnp.int32, (q_len, q_len), 1)
    tri = si <= ti
    tri_b = jnp.where(tri, 1.0, 0.0).astype(BF16)
    tri_tb = jnp.where(ti <= si, 1.0, 0.0).astype(BF16)

    dt = jax.nn.softplus(dtr_ref[...] + dtb_ref[...])
    dtt = jax.nn.softplus(dtrt_ref[...] + dtbc_ref[...])
    if t_valid < q_len:
        dt = jnp.where(lax.broadcasted_iota(jnp.int32, dt.shape, 0) < t_valid, dt, 0.0)
        dtt = jnp.where(lax.broadcasted_iota(jnp.int32, dtt.shape, 1) < t_valid, dtt, 0.0)
    da = dt * (-jnp.exp(alog_ref[...]))
    dat = dtt * (-jnp.exp(alogc_ref[...]))
    a_cum = functools.reduce(jnp.add, [_dot(tri_b, p) for p in reversed(_split3(da))])
    a_cumt = functools.reduce(jnp.add, [_dot(p, tri_tb) for p in reversed(_split3(dat))])
    dt_parts = _split3(dt)
    ac_parts = _split3(a_cum)

    lane_lo = lax.broadcasted_iota(jnp.int32, (q_len, LANES), 1) < C_HEAD_DIM

    for g in range(C_GROUPS):
        gs = slice(g * C_GW, (g + 1) * C_GW)
        e_g = e_ref[:, gs]
        dt_x = functools.reduce(jnp.add, [_dot(p, e_g) for p in reversed(dt_parts)])
        ac_x = functools.reduce(jnp.add, [_dot(p, e_g) for p in reversed(ac_parts)])
        xs = act_scr[:, gs]
        bm = act_scr[:, C_D_INNER + g * C_D_STATE: C_D_INNER + (g + 1) * C_D_STATE]
        cm = act_scr[:, C_D_INNER + C_GROUPS * C_D_STATE + g * C_D_STATE:
                     C_D_INNER + C_GROUPS * C_D_STATE + (g + 1) * C_D_STATE]
        bm_b = bm.astype(BF16)
        cm_b = cm.astype(BF16)
        xdt = xs * dt_x
        a_last = ac_x[q_len - 1:q_len, :]
        xdt_b = xdt.astype(BF16)
        xdt_end_b = (xdt * jnp.exp(a_last - ac_x)).astype(BF16)
        cbm = _dot_nt(cm_b, bm_b)

        h_prev = h_ref[0, g]
        y = _dot(cm_b, h_prev.astype(BF16)) * jnp.exp(ac_x)
        st = _dot(jnp.transpose(bm).astype(BF16), xdt_end_b)
        h_ref[0, g] = h_prev * jnp.exp(a_last) + st

        pairs = []
        for hp in range(C_GW // LANES):
            ms = []
            for hh in (2 * hp, 2 * hp + 1):
                hd = g * (C_GW // C_HEAD_DIM) + hh
                seg = a_cum[:, hd:hd + 1] - a_cumt[hd:hd + 1, :]
                decay = jnp.exp(jnp.where(tri, seg, -jnp.inf))
                ms.append((cbm * decay).astype(BF16))
            slab = xdt_b[:, hp * LANES:(hp + 1) * LANES]
            zero = jnp.zeros_like(slab)
            rhs = jnp.concatenate([jnp.where(lane_lo, slab, zero), jnp.where(lane_lo, zero, slab)], axis=0)
            pairs.append(_dot(jnp.concatenate(ms, axis=1), rhs))
        y = y + jnp.concatenate(pairs, axis=1)
        y = y + xs * dsk_ref[:, gs]
        zz = z_ref[:, gs]
        y = y * (zz * jax.nn.sigmoid(zz))
        y = y * lax.rsqrt(jnp.mean(y * y, axis=-1, keepdims=True) + EPS) * ng_ref[:, gs]
        y_ref[:, gs] = y.astype(BF16)


def _ssd(proj, dt_raw, conv_state, h0t, conv_w, conv_b, dt_bias, a_log, d_skip, norm_g,
         *, batch, n_chunks, q_len, t_valid):
    rows = batch * n_chunks * q_len
    dtrt = dt_raw.reshape(batch * n_chunks, q_len, LANES).transpose(0, 2, 1).reshape(-1, q_len)
    pad = LANES - C_HEADS
    col = lambda v: jnp.pad(v, (0, pad)).reshape(LANES, 1)
    row = lambda v: jnp.pad(v, (0, pad)).reshape(1, LANES)
    expand = jnp.repeat(jnp.eye(LANES, C_HEADS, dtype=BF16), C_HEAD_DIM, axis=1)
    cw = jnp.pad(conv_w, ((0, CONV_PAD - C_CONV), (0, 0)))
    xcol = C_D_INNER // (C_CONV_DIM // 3)
    const = lambda shape: pl.BlockSpec(shape, lambda b, c: (0,) * len(shape))
    blk = lambda width, j: pl.BlockSpec((q_len, width), lambda b, c: (b * n_chunks + c, j))
    y, h = pl.pallas_call(
        functools.partial(_ssd_kernel, q_len=q_len, t_valid=t_valid),
        grid=(batch, n_chunks),
        in_specs=[
            blk(C_D_INNER, 0), blk(C_CONV_DIM // 3, xcol), blk(C_CONV_DIM // 3, xcol + 1),
            blk(C_CONV_DIM // 3, xcol + 2), blk(LANES, 0),
            pl.BlockSpec((LANES, q_len), lambda b, c: (b * n_chunks + c, 0)),
            pl.BlockSpec((1, CONV_PAD, C_CONV_DIM), lambda b, c: (b, 0, 0)),
            const((CONV_PAD, C_CONV_DIM)), const((1, C_CONV_DIM)),
            const((1, LANES)), const((LANES, 1)), const((1, LANES)), const((LANES, 1)),
            const((1, C_D_INNER)), const((1, C_D_INNER)), const((LANES, C_D_INNER)),
            pl.BlockSpec((1, C_GROUPS, C_D_STATE, C_GW), lambda b, c: (b, 0, 0, 0)),
        ],
        out_specs=[
            pl.BlockSpec((q_len, C_D_INNER), lambda b, c: (b * n_chunks + c, 0)),
            pl.BlockSpec((1, C_GROUPS, C_D_STATE, C_GW), lambda b, c: (b, 0, 0, 0)),
        ],
        out_shape=[
            jax.ShapeDtypeStruct((rows, C_D_INNER), BF16),
            jax.ShapeDtypeStruct((batch, C_GROUPS, C_D_STATE, C_GW), F32),
        ],
        scratch_shapes=[
            pltpu.VMEM((q_len + CONV_PAD, C_CONV_DIM), F32),
            pltpu.VMEM((q_len, C_CONV_DIM), F32),
        ],
        compiler_params=_cparams(("parallel", "arbitrary")),
        name="ssd",
    )(proj, proj, proj, proj, dt_raw, dtrt, conv_state, cw, conv_b.reshape(1, -1),
      row(dt_bias), col(dt_bias), row(a_log), col(a_log),
      jnp.repeat(d_skip, C_HEAD_DIM).reshape(1, -1), norm_g.reshape(1, -1), expand, h0t)
    return y, h


def _state_to_t(h):
    b = h.shape[0]
    return h.reshape(b, C_GROUPS, C_HEADS // C_GROUPS, C_HEAD_DIM, C_D_STATE).transpose(0, 1, 4, 2, 3).reshape(
        b, C_GROUPS, C_D_STATE, C_GW)


def _state_from_t(ht):
    b = ht.shape[0]
    return ht.reshape(b, C_GROUPS, C_D_STATE, C_HEADS // C_GROUPS, C_HEAD_DIM).transpose(0, 1, 3, 4, 2).reshape(
        b, C_HEADS, C_HEAD_DIM, C_D_STATE)


def _router_kernel(x_ref, g_ref, w_ref, b_ref, base_ref, *rest, n_tiles):
    h_ref, mi_ref, mf_ref, cnt_ref = rest[-4:]
    i = pl.program_id(0)

    @pl.when(i == 0)
    def _():
        cnt_ref[...] = base_ref[...]

    @pl.when(i < n_tiles)
    def _():
        _route_tile(x_ref, g_ref, w_ref, b_ref, h_ref, mi_ref, mf_ref, cnt_ref)

    @pl.when(i >= n_tiles)
    def _():
        h_ref[...] = jnp.zeros_like(h_ref)


def _route_tile(x_ref, g_ref, w_ref, b_ref, h_ref, mi_ref, mf_ref, cnt_ref):
    tm = x_ref.shape[0]
    h = _rmsnorm_rows(x_ref[...], g_ref[...])
    h_ref[...] = h
    h_hi, h_lo, _ = _split3(h)
    w_hi, w_lo, _ = _split3(w_ref[...])
    logits = (_dot(h_lo, w_hi) + _dot(h_hi, w_lo)) + _dot(h_hi, w_hi) + b_ref[...]

    lane = lax.broadcasted_iota(jnp.int32, (tm, LANES), 1).astype(F32)
    big = float(4 * LANES)
    lg = jnp.where(lane < E_GROUPS, logits, -jnp.inf)
    mg = jnp.max(lg, axis=-1, keepdims=True)
    p_top = 1.0 / jnp.sum(jnp.exp(lg - mg), axis=-1, keepdims=True)
    g_top = jnp.min(jnp.where(lg == mg, lane, big), axis=-1, keepdims=True)
    lo = E_GROUPS + g_top * E_PER_GROUP
    le = jnp.where(jnp.logical_and(lane >= lo, lane < lo + E_PER_GROUP), logits, -jnp.inf)
    v1 = jnp.max(le, axis=-1, keepdims=True)
    i1 = jnp.min(jnp.where(le == v1, lane, big), axis=-1, keepdims=True)
    le2 = jnp.where(lane == i1, -jnp.inf, le)
    v2 = jnp.max(le2, axis=-1, keepdims=True)
    i2 = jnp.min(jnp.where(le2 == v2, lane, big), axis=-1, keepdims=True)
    e21 = jnp.exp(v2 - v1)
    gate1 = p_top / (1.0 + e21)
    gate2 = p_top * e21 / (1.0 + e21)
    eid1 = i1 - E_GROUPS
    eid2 = i2 - E_GROUPS

    oh1 = lane == eid1
    oh2 = lane == eid2
    oh = jnp.where(jnp.logical_or(oh1, oh2), 1.0, 0.0)
    ri = lax.broadcasted_iota(jnp.int32, (tm, tm), 0)
    ci = lax.broadcasted_iota(jnp.int32, (tm, tm), 1)
    before = jnp.where(ci < ri, 1.0, 0.0).astype(BF16)
    pos = _dot(before, oh.astype(BF16)) + cnt_ref[...]
    rank1 = jnp.sum(jnp.where(oh1, pos, 0.0), axis=-1, keepdims=True)
    rank2 = jnp.sum(jnp.where(oh2, pos, 0.0), axis=-1, keepdims=True)
    cnt_ref[...] = cnt_ref[...] + jnp.sum(oh, axis=0, keepdims=True)

    mi = jnp.where(lane == 0, eid1, jnp.where(lane == 1, eid2, jnp.where(lane == 2, rank1,
                                                                         jnp.where(lane == 3, rank2, 0.0))))
    sel = jnp.where(lax.broadcasted_iota(jnp.int32, (META_ROWS, LANES), 0)
                    == lax.broadcasted_iota(jnp.int32, (META_ROWS, LANES), 1), 1.0, 0.0).astype(BF16)
    mi_t = functools.reduce(jnp.add, [_dot_nt(sel, p) for p in _split3(mi)])
    mi_ref[...] = mi_t.astype(jnp.int32)
    mf_ref[...] = jnp.where(lane == 0, gate1, jnp.where(lane == 1, gate2, 0.0))


def _router(x, g, w_all, b_all, base, *, tm, h_into=None, h_row0=0):
    m = x.shape[0]
    n_tiles = m // tm
    last = n_tiles - 1
    tile = lambda i: jnp.minimum(i, last)
    one = pl.BlockSpec((1, LANES), lambda i: (0, 0))
    in_specs = [
        pl.BlockSpec((tm, D_MODEL), lambda i: (tile(i), 0)),
        pl.BlockSpec((1, D_MODEL), lambda i: (0, 0)),
        pl.BlockSpec((D_MODEL, LANES), lambda i: (0, 0)),
        one, one,
    ]
    args = [x, g.reshape(1, D_MODEL), w_all, b_all, base]
    if h_into is None:
        steps, h_rows, aliases = n_tiles + 1, (n_tiles + 1) * tm, {}
    else:
        steps, h_rows, aliases = n_tiles, h_into.shape[0], {len(args): 0}
        in_specs.append(pl.BlockSpec(memory_space=pl.ANY))
        args.append(h_into)
    return pl.pallas_call(
        functools.partial(_router_kernel, n_tiles=n_tiles),
        grid=(steps,),
        in_specs=in_specs,
        out_specs=[pl.BlockSpec((tm, D_MODEL), lambda i: (h_row0 // tm + i, 0)),
                   pl.BlockSpec((META_ROWS, tm), lambda i: (0, tile(i))),
                   pl.BlockSpec((tm, LANES), lambda i: (tile(i), 0)), one],
        out_shape=[
            jax.ShapeDtypeStruct((h_rows, D_MODEL), F32),
            jax.ShapeDtypeStruct((META_ROWS, m), jnp.int32),
            jax.ShapeDtypeStruct((m, LANES), F32),
            jax.ShapeDtypeStruct((1, LANES), F32),
        ],
        input_output_aliases=aliases,
        compiler_params=_cparams(("arbitrary",)),
        name="moe_router",
    )(*args)


def _gather_rows(idx_ref, base, n_rows, src_hbm, dst, sem):
    for r in range(n_rows):
        t = idx_ref[base + r]
        pltpu.make_async_copy(src_hbm.at[pl.ds(t, 1)], dst.at[pl.ds(r, 1)], sem).start()


def _wait_rows(n_rows, src_hbm, dst, sem):
    pltpu.make_async_copy(src_hbm.at[pl.ds(0, n_rows)], dst, sem).wait()


def _moe_ffn_kernel(be_ref, nu_ref, nv_ref, tok_ref, h_hbm, wg_hbm, wu_hbm, wd_hbm, y_ref,
                    xbuf, wg_buf, wu_buf, wd_buf, sem, wsem, ord_scr, *, layer):
    i = pl.program_id(0)
    n_used = nu_ref[0]
    n_groups = MOE_ROWS // GATHER_ROWS
    w_pairs = ((wg_hbm, wg_buf), (wu_hbm, wu_buf), (wd_hbm, wd_buf))

    def weights(expert, wslot, start):
        for k, (src, dst) in enumerate(w_pairs):
            cp = pltpu.make_async_copy(src.at[layer, expert], dst.at[wslot], wsem.at[wslot, k])
            if start:
                cp.start(priority=1)
            else:
                cp.wait()

    first = jnp.logical_and(i < n_used,
                            jnp.logical_or(i == 0, be_ref[i] != be_ref[jnp.maximum(i - 1, 0)]))

    @pl.when(i == 0)
    def _():
        ord_scr[0] = -1
        weights(be_ref[0], 0, True)

    @pl.when(first)
    def _():
        ordinal = ord_scr[0] + 1
        ord_scr[0] = ordinal
        nxt = lax.while_loop(lambda j: jnp.logical_and(j < n_used, be_ref[jnp.minimum(j, n_used - 1)] == be_ref[i]),
                             lambda j: j + 1, i + 1)

        @pl.when(nxt < n_used)
        def _():
            weights(be_ref[nxt], (ordinal + 1) % 2, True)
        weights(be_ref[i], ordinal % 2, False)

    def fetch(block):
        slot = block % 2
        for grp in range(n_groups):
            @pl.when(grp * GATHER_ROWS < nv_ref[block])
            def _():
                _gather_rows(tok_ref, block * MOE_ROWS + grp * GATHER_ROWS, GATHER_ROWS, h_hbm,
                             xbuf.at[slot, pl.ds(grp * GATHER_ROWS, GATHER_ROWS)], sem.at[slot, grp])

    @pl.when(i == 0)
    def _():
        xbuf[...] = jnp.zeros_like(xbuf)
        fetch(0)

    @pl.when(i + 1 < n_used)
    def _():
        fetch(i + 1)

    @pl.when(i < n_used)
    def _():
        slot = i % 2
        for grp in range(n_groups):
            @pl.when(grp * GATHER_ROWS < nv_ref[i])
            def _():
                _wait_rows(GATHER_ROWS, h_hbm, xbuf.at[slot, pl.ds(grp * GATHER_ROWS, GATHER_ROWS)],
                           sem.at[slot, grp])
        wslot = ord_scr[0] % 2
        x = xbuf[slot].astype(BF16)
        gate = _dot(x, wg_buf[wslot].astype(BF16))
        up = _dot(x, wu_buf[wslot].astype(BF16))
        act = (gate * jax.nn.sigmoid(gate) * up).astype(BF16)
        y_ref[...] = _dot(act, wd_buf[wslot].astype(BF16))

    @pl.when(i >= n_used)
    def _():
        y_ref[...] = jnp.zeros_like(y_ref)


def _moe_ffn(h_all, plan, w_gate, w_up, w_down, layer):
    blk_expert, n_used, n_valid, tok_of_slot = plan
    n_blocks = tok_of_slot.shape[0] // MOE_ROWS
    rows = lambda i, be, nu, nv, tok: (i, 0)
    hbm = pl.BlockSpec(memory_space=pl.ANY)
    return pl.pallas_call(
        functools.partial(_moe_ffn_kernel, layer=layer),
        grid_spec=pltpu.PrefetchScalarGridSpec(
            num_scalar_prefetch=4,
            grid=(n_blocks,),
            in_specs=[hbm, hbm, hbm, hbm],
            out_specs=pl.BlockSpec((MOE_ROWS, D_MODEL), rows),
            scratch_shapes=[pltpu.VMEM((2, MOE_ROWS, D_MODEL), F32),
                            pltpu.VMEM((2, D_MODEL, E_FF), F32), pltpu.VMEM((2, D_MODEL, E_FF), F32),
                            pltpu.VMEM((2, E_FF, D_MODEL), F32),
                            pltpu.SemaphoreType.DMA((2, MOE_ROWS // GATHER_ROWS)),
                            pltpu.SemaphoreType.DMA((2, 3)), pltpu.SMEM((1,), jnp.int32)],
        ),
        out_shape=jax.ShapeDtypeStruct((n_blocks * MOE_ROWS, D_MODEL), F32),
        compiler_params=_cparams(("arbitrary",), unchecked_dma=True),
        name="moe_ffn",
    )(blk_expert, n_used, n_valid, tok_of_slot, h_all, w_gate, w_up, w_down)


def _moe_plan_kernel(meta_ref, cnt_ref, be_ref, nu_ref, nv_ref, tok_ref, dp_ref, ds_ref, start_scr,
                     *, n_p, n_s, n_blocks):
    n_tok = n_p + n_s

    def segment(e, first_blk):
        start_scr[e] = first_blk * MOE_ROWS
        cnt = cnt_ref[e]
        nb = (cnt + MOE_ROWS - 1) // MOE_ROWS

        def fill(j, carry):
            be_ref[first_blk + j] = e
            nv_ref[first_blk + j] = jnp.minimum(cnt - j * MOE_ROWS, MOE_ROWS)
            return carry
        lax.fori_loop(0, nb, fill, 0)
        return first_blk + nb
    n_used = lax.fori_loop(0, N_EXPERTS, segment, 0)
    nu_ref[0] = n_used

    last_expert = be_ref[jnp.maximum(n_used - 1, 0)]

    def tail(j, carry):
        be_ref[j] = last_expert
        nv_ref[j] = 0
        return carry
    lax.fori_loop(n_used, n_blocks, tail, 0)

    def clear(j, carry):
        tok_ref[j] = 0
        return carry
    lax.fori_loop(0, n_blocks * MOE_ROWS, clear, 0, unroll=8)

    def place(dest_ref, first, count):
        def body(j, carry):
            t = first + j
            for k in range(2):
                d = start_scr[meta_ref[k * n_tok + t]] + meta_ref[(2 + k) * n_tok + t]
                tok_ref[d] = t
                dest_ref[k * count + j] = d
            return carry
        lax.fori_loop(0, count, body, 0, unroll=4)
    place(dp_ref, 0, n_p)
    place(ds_ref, n_p, n_s)


def _moe_plan(meta, counts, *, n_p, n_s):
    n_blocks = (2 * (n_p + n_s)) // MOE_ROWS + N_EXPERTS
    smem = pl.BlockSpec(memory_space=pltpu.SMEM)
    i32 = lambda n: jax.ShapeDtypeStruct((n,), jnp.int32)
    be, nu, nv, tok, dp, ds = pl.pallas_call(
        functools.partial(_moe_plan_kernel, n_p=n_p, n_s=n_s, n_blocks=n_blocks),
        in_specs=[smem, smem],
        out_specs=[smem] * 6,
        out_shape=[i32(n_blocks), i32(1), i32(n_blocks), i32(n_blocks * MOE_ROWS), i32(2 * n_p), i32(2 * n_s)],
        scratch_shapes=[pltpu.SMEM((N_EXPERTS,), jnp.int32)],
        name="moe_plan",
    )(meta, counts)
    return (be, nu, nv, tok), dp, ds


def _moe_combine_kernel(dest_ref, x_ref, g_ref, y_hbm, o_ref, ybuf, sem, *, tm, n_tiles):
    i = pl.program_id(0)

    def fetch(tile):
        slot = tile % 2
        for k in range(2):
            _gather_rows(dest_ref, (k * n_tiles + tile) * tm, tm, y_hbm, ybuf.at[slot, k], sem.at[slot, k])

    @pl.when(i == 0)
    def _():
        fetch(0)

    @pl.when(i + 1 < n_tiles)
    def _():
        fetch(i + 1)

    slot = i % 2
    acc = x_ref[...]
    for k in range(2):
        _wait_rows(tm, y_hbm, ybuf.at[slot, k], sem.at[slot, k])
        acc = acc + g_ref[:, k:k + 1] * ybuf[slot, k]
    o_ref[...] = acc


def _moe_combine(x, gates, dest, y_rows, *, tm):
    m = x.shape[0]
    n_tiles = m // tm
    return pl.pallas_call(
        functools.partial(_moe_combine_kernel, tm=tm, n_tiles=n_tiles),
        grid_spec=pltpu.PrefetchScalarGridSpec(
            num_scalar_prefetch=1,
            grid=(n_tiles,),
            in_specs=[
                pl.BlockSpec((tm, D_MODEL), lambda i, d: (i, 0)),
                pl.BlockSpec((tm, LANES), lambda i, d: (i, 0)),
                pl.BlockSpec(memory_space=pl.ANY),
            ],
            out_specs=pl.BlockSpec((tm, D_MODEL), lambda i, d: (i, 0)),
            scratch_shapes=[pltpu.VMEM((2, 2, tm, D_MODEL), F32), pltpu.SemaphoreType.DMA((2, 2))],
        ),
        out_shape=jax.ShapeDtypeStruct((m, D_MODEL), F32),
        compiler_params=_cparams(("arbitrary",), unchecked_dma=True),
        name="moe_combine",
    )(dest, x, gates, y_rows)


def _moe(xp, xs, g, w_rg, b_rg, w_re, b_re, w_gate, w_up, w_down, layer):
    n_p, n_s = xp.shape[0], xs.shape[0]
    pad = LANES - E_GROUPS - N_EXPERTS
    w_all = jnp.pad(jnp.concatenate([w_rg, w_re.reshape(D_MODEL, N_EXPERTS)], axis=1), ((0, 0), (0, pad)))
    b_all = jnp.pad(jnp.concatenate([b_rg, b_re.reshape(N_EXPERTS)]), (0, pad)).reshape(1, LANES)
    h_all, mip, mfp, cnt_p = _router(xp, g, w_all, b_all, jnp.zeros((1, LANES), F32), tm=512)
    h_all, mis, mfs, cnt = _router(xs, g, w_all, b_all, cnt_p, tm=n_s, h_into=h_all, h_row0=n_p)
    meta = jnp.concatenate([mip[:4], mis[:4]], axis=1).reshape(-1)
    plan, dest_p, dest_s = _moe_plan(meta, cnt[0].astype(jnp.int32), n_p=n_p, n_s=n_s)
    y_rows = _moe_ffn(h_all, plan, w_gate, w_up, w_down, layer)
    return (_moe_combine(xp, mfp, dest_p, y_rows, tm=COMBINE_ROWS),
            _moe_combine(xs, mfs, dest_s, y_rows, tm=n_s))


def _final_norm_kernel(x_ref, g_ref, o_ref):
    o_ref[...] = _rmsnorm_rows(x_ref[...], g_ref[...])


def _final_norm(x, g, *, tm):
    m = x.shape[0]
    return pl.pallas_call(
        _final_norm_kernel,
        grid=(m // tm,),
        in_specs=[pl.BlockSpec((tm, D_MODEL), lambda i: (i, 0)), pl.BlockSpec((1, D_MODEL), lambda i: (0, 0))],
        out_specs=pl.BlockSpec((tm, D_MODEL), lambda i: (i, 0)),
        out_shape=jax.ShapeDtypeStruct((m, D_MODEL), F32),
        compiler_params=_cparams(("parallel",)),
        name="final_norm",
    )(x, g.reshape(1, D_MODEL))


def _rope_tables(pos):
    half = HEAD_DIM // 2
    inv = 1.0 / (ROPE_THETA ** (jnp.arange(half, dtype=F32) / half))
    ang = pos.astype(F32)[:, None] * inv[None, :]
    cos, sin = jnp.cos(ang), jnp.sin(ang)
    return jnp.concatenate([cos, cos], -1), jnp.concatenate([-sin, sin], -1)


def _kv_state(proj3, group, keep):
    b, t, _ = proj3.shape
    base = 2 * A_WIDTH + group * 3 * B_HW
    k = proj3[:, t - keep:, base + B_HW: base + 2 * B_HW]
    v = proj3[:, t - keep:, base + 2 * B_HW: base + 3 * B_HW]
    return jnp.stack([k, v], axis=2).reshape(b, keep, 2, B_HEADS, HEAD_DIM)


def _layer_ab(xp, xs, g, w_in_b, ln_g, ln_b, w_s, b_s, w_out_b, cache_views, layer, tabs_p, tabs_s, shapes):
    bp, tp, bs, ts = shapes

    proj_p = _inproj(xp, g, w_in_b, layer, tabs_p, tm=1024)
    a_p = _gmlp(proj_p, ln_g, ln_b, w_s, b_s, n_rows=bp * tp, chunks=2, emit_v=False)[0]
    outs, lses = zip(*[_attn_prompt(proj_p, grp, batch=bp, seq=tp) for grp in range(len(B_CONFIGS))])
    b_p = _merge(outs, lses, tm=1024)
    xp_new = _outproj([a_p, b_p], w_out_b, layer, xp, tm=1024, tn=1024)
    proj_p3 = proj_p.reshape(bp, tp, AB_IN)
    kv_p = [_kv_state(proj_p3, grp, min(w, tp)) for grp, (w, _) in enumerate(B_CONFIGS)]

    n_s = bs * ts
    proj_s = _inproj(xs, g, w_in_b, layer, tabs_s, tm=n_s)
    proj_s3 = proj_s.reshape(bs, ts, AB_IN)
    chunk_in = jnp.pad(proj_s3[:, :, :2 * A_WIDTH], ((0, 0), (0, A_CHUNK - ts), (0, 0)))
    a_s, v_s = _gmlp(chunk_in.reshape(bs * A_CHUNK, 2 * A_WIDTH), ln_g, ln_b, w_s, b_s,
                     n_rows=bs * A_CHUNK, chunks=1, emit_v=True)
    a_s = a_s.reshape(bs, A_CHUNK, A_WIDTH)[:, :ts].reshape(n_s, A_WIDTH)
    v_s = v_s.reshape(bs, A_CHUNK, A_WIDTH)[:, :ts]
    qkv_s = jnp.pad(proj_s3[:, :, 2 * A_WIDTH:], ((0, 0), (0, SAMPLE_ROWS - ts), (0, 0)))
    b_s_out = _attn_sample(qkv_s, cache_views, layer)[:, :ts].reshape(n_s, B_HW)
    xs_new = _outproj([a_s, b_s_out], w_out_b, layer, xs, tm=n_s, tn=1024)
    kv_s = [_kv_state(proj_s3, grp, ts) for grp in range(len(B_CONFIGS))]
    return xp_new, xs_new, kv_p, kv_s, v_s


def _layer_c(xp, xs, g, w_main, w_dt, conv_w, conv_b, dt_bias, a_log, d_skip, norm_g, w_out_b, layer,
             conv_state, ssm_state, shapes):
    bp, tp, bs, ts = shapes
    n_main = C_D_INNER + C_CONV_DIM
    args = (conv_w, conv_b, dt_bias, a_log, d_skip, norm_g)

    proj_p = _inproj(xp, g, w_main, layer, tm=1024, n_out=n_main)
    dt_p = _inproj(xp, g, w_dt, layer, tm=1024)
    zc = jnp.zeros((bp, CONV_PAD, C_CONV_DIM), F32)
    zh = jnp.zeros((bp, C_GROUPS, C_D_STATE, C_GW), F32)
    q_p = 128
    y_p, h_p = _ssd(proj_p, dt_p, zc, zh, *args, batch=bp, n_chunks=tp // q_p, q_len=q_p, t_valid=q_p)
    xp_new = _outproj([y_p], w_out_b, layer, xp, tm=512, tn=1024)
    conv_p = proj_p.reshape(bp, tp, n_main)[:, tp - (C_CONV - 1):, C_D_INNER:]

    n_s = bs * ts
    q_s = 16
    proj_s = _inproj(xs, g, w_main, layer, tm=n_s, n_out=n_main)
    dt_s = _inproj(xs, g, w_dt, layer, tm=n_s)
    pad_rows = lambda a: jnp.pad(a.reshape(bs, ts, -1), ((0, 0), (0, q_s - ts), (0, 0))).reshape(bs * q_s, -1)
    cst = jnp.pad(conv_state, ((0, 0), (CONV_PAD - (C_CONV - 1), 0), (0, 0)))
    y_s, h_s = _ssd(pad_rows(proj_s), pad_rows(dt_s), cst, _state_to_t(ssm_state), *args,
                    batch=bs, n_chunks=1, q_len=q_s, t_valid=ts)
    y_s = y_s.reshape(bs, q_s, C_D_INNER)[:, :ts].reshape(n_s, C_D_INNER)
    xs_new = _outproj([y_s], w_out_b, layer, xs, tm=n_s, tn=1024)
    assert ts >= C_CONV - 1
    conv_s = proj_s.reshape(bs, ts, n_main)[:, ts - (C_CONV - 1):, C_D_INNER:]
    return xp_new, xs_new, conv_p, conv_s, _state_from_t(h_p), _state_from_t(h_s)


def kernel(x_prompt, x_sample, cache_kv_w128, cache_kv_w512, cache_kv_w2048, state_conv, state_ssm,
           norm_mix, norm_ffn, norm_final, w_in_ab, a_ln_g, a_ln_b, a_w_s, a_b_s, w_out_ab,
           w_in_c, c_conv_w, c_conv_b, c_dt_bias, c_a_log, c_d, c_norm_g, w_out_c,
           w_router_g, b_router_g, w_router_e, b_router_e, w_exp_gate, w_exp_up, w_exp_down):
    bp, tp, _ = x_prompt.shape
    bs, ts, _ = x_sample.shape
    shapes = (bp, tp, bs, ts)
    depth = norm_mix.shape[0]
    kv_caches = (cache_kv_w128, cache_kv_w512, cache_kv_w2048)
    tabs_p = _rope_tables(jnp.arange(tp, dtype=jnp.int32))
    pos_s = PAST_LEN + jnp.arange(ts, dtype=jnp.int32)
    tabs_s = tuple(jnp.tile(t, (bs, 1)) for t in _rope_tables(pos_s))

    w_in_ab_b = w_in_ab.astype(BF16)
    w_out_ab_b = w_out_ab.astype(BF16)
    w_in_c_b = w_in_c[:, :, :C_D_INNER + C_CONV_DIM].astype(BF16)
    w_dt_b = jnp.pad(w_in_c[:, :, C_D_INNER + C_CONV_DIM:], ((0, 0), (0, 0), (0, LANES - C_HEADS))).astype(BF16)
    w_out_c_b = w_out_c.astype(BF16)
    cache_views = _cache_views(kv_caches, ts)

    xp = x_prompt.reshape(bp * tp, D_MODEL)
    xs = x_sample.reshape(bs * ts, D_MODEL)
    kv_p = [[] for _ in B_CONFIGS]
    kv_s = [[] for _ in B_CONFIGS]
    chunk_v, conv_p, conv_s, ssm_p, ssm_s = [], [], [], [], []
    for l in range(depth):
        i = l // 2
        if l % 2 == 0:
            xp, xs, nkv_p, nkv_s, v_s = _layer_ab(
                xp, xs, norm_mix[l], w_in_ab_b, a_ln_g[i], a_ln_b[i], a_w_s[i], a_b_s[i], w_out_ab_b,
                cache_views, i, tabs_p, tabs_s, shapes)
            for grp in range(len(B_CONFIGS)):
                kv_p[grp].append(nkv_p[grp])
                kv_s[grp].append(nkv_s[grp])
            chunk_v.append(v_s)
        else:
            xp, xs, ncp, ncs, nsp, nss = _layer_c(
                xp, xs, norm_mix[l], w_in_c_b, w_dt_b, c_conv_w[i], c_conv_b[i], c_dt_bias[i], c_a_log[i], c_d[i],
                c_norm_g[i], w_out_c_b, i, state_conv[i], state_ssm[i], shapes)
            conv_p.append(ncp)
            conv_s.append(ncs)
            ssm_p.append(nsp)
            ssm_s.append(nss)
        xp, xs = _moe(xp, xs, norm_ffn[l], w_router_g[l], b_router_g[l], w_router_e[l], b_router_e[l],
                      w_exp_gate, w_exp_up, w_exp_down, l)
    y_p = _final_norm(xp, norm_final, tm=1024).reshape(bp, tp, D_MODEL)
    y_s = _final_norm(xs, norm_final, tm=bs * ts).reshape(bs, ts, D_MODEL)
    kv_s = [jnp.concatenate([c[:, :, ts:], jnp.stack(new)], axis=2) for c, new in zip(kv_caches, kv_s)]
    return (y_p, y_s,
            jnp.stack(kv_p[0]), kv_s[0],
            jnp.stack(kv_p[1]), kv_s[1],
            jnp.stack(kv_p[2]), kv_s[2],
            jnp.stack(chunk_v),
            jnp.stack(conv_p), jnp.stack(conv_s),
            jnp.stack(ssm_p), jnp.stack(ssm_s))
```
